```python
import jax
import jax.numpy as jnp
from jax import lax
import numpy as np

D_MODEL = 2048
BATCH = 4
SEQ = 4096
DEPTH = 1
DEC_BATCH = 128
DEC_SEQ = 4
PAST_LEN = 16384
PAGE_SIZE = 128

MLA_HEADS = 8
Q_LORA = 512
KV_LORA = 512
NOPE_DIM = 128
ROPE_DIM = 64
V_DIM = 128
ROPE_THETA = 10000.0
HEAD_DIM = 128
MOBA_HEADS = 8
MOBA_KV_HEADS = 2
MOBA_BLOCK = 256
MOBA_TOPK = 3
N_EXPERTS = 32
TOP_K = 4
D_FF = 2048
SWIGLU_LIMIT = 7.0
SWIGLU_ALPHA = 1.702
NORM_EPS = 1e-6
MLA_Q_BLOCK = 128
MOBA_Q_CHUNK = 16
MOE_ROW_BLOCK = 128
NEG_INF = -1e30
IN_SIZES = (Q_LORA, KV_LORA, ROPE_DIM, MOBA_HEADS * HEAD_DIM, MOBA_KV_HEADS * HEAD_DIM, MOBA_KV_HEADS * HEAD_DIM, D_MODEL, D_MODEL)
IN_WIDTH = Q_LORA + KV_LORA + ROPE_DIM + (MOBA_HEADS + 2 * MOBA_KV_HEADS) * HEAD_DIM + 2 * D_MODEL

kernel_name = 'mla_moba_moe_hybrid_step'


def rmsnorm(x, g):
    x32 = x.astype(jnp.float32)
    y = x32 * lax.rsqrt(jnp.mean(x32 * x32, axis=-1, keepdims=True) + NORM_EPS)
    return (y * g.astype(jnp.float32)).astype(x.dtype)


def rope(x, pos):
    half = ROPE_DIM // 2
    inv = ROPE_THETA ** (-jnp.arange(half, dtype=jnp.float32) / half)
    ang = pos.astype(jnp.float32)[:, None] * inv[None, :]
    shape = (pos.shape[0],) + (1,) * (x.ndim - 3) + (half,)
    cos, sin = jnp.cos(ang).reshape(shape), jnp.sin(ang).reshape(shape)
    x32 = x.astype(jnp.float32)
    x1, x2 = x32[..., :half], x32[..., half:]
    return jnp.concatenate([x1 * cos - x2 * sin, x2 * cos + x1 * sin], axis=-1).astype(x.dtype)


def alibi_slopes():
    return 2.0 ** (-(8.0 / MOBA_HEADS) * jnp.arange(1, MOBA_HEADS + 1, dtype=jnp.float32))


def split_cols(proj):
    out, start = [], 0
    for n in IN_SIZES:
        out.append(proj[..., start:start + n])
        start += n
    return out


def project_inputs(h, pos, w_in, g_q, w_q_up, g_kv, w_uk):
    b, s = h.shape[:2]
    q_lat, c_raw, kpe_raw, mq, mk, mv, ga, gb = split_cols(h @ w_in)
    q = (rmsnorm(q_lat, g_q) @ w_q_up).reshape(b, s, MLA_HEADS, NOPE_DIM + ROPE_DIM)
    q_abs = jnp.einsum('bshd,chd->bshc', q[..., :NOPE_DIM], w_uk)
    q_rope = rope(q[..., NOPE_DIM:], pos)
    ckv = rmsnorm(c_raw, g_kv)
    kpe = rope(kpe_raw, pos)
    mq = mq.reshape(b, s, MOBA_HEADS, HEAD_DIM)
    mk = mk.reshape(b, s, MOBA_KV_HEADS, HEAD_DIM)
    mv = mv.reshape(b, s, MOBA_KV_HEADS, HEAD_DIM)
    return q_abs, q_rope, ckv, kpe, mq, mk, mv, ga, gb


def latent_attend(q_abs, q_rope, q_pos, ckv, kpe, k_pos):
    scale = (NOPE_DIM + ROPE_DIM) ** -0.5
    s = jnp.einsum('bqhc,bkc->bhqk', q_abs, ckv) + jnp.einsum('bqhr,bkr->bhqk', q_rope, kpe)
    s = jnp.where(k_pos[None, :] <= q_pos[:, None], s.astype(jnp.float32) * scale, NEG_INF)
    p = jax.nn.softmax(s, axis=-1).astype(ckv.dtype)
    return jnp.einsum('bhqk,bkc->bqhc', p, ckv)


def moba_prepare(k, v):
    length = k.shape[0]
    nb = -(-length // MOBA_BLOCK)
    pad = nb * MOBA_BLOCK - length
    k_pad = jnp.pad(k, ((0, pad), (0, 0), (0, 0)))
    v_pad = jnp.pad(v, ((0, pad), (0, 0), (0, 0)))
    means = k_pad.reshape(nb, MOBA_BLOCK, MOBA_KV_HEADS, HEAD_DIM).astype(jnp.float32).mean(axis=1)
    return k_pad, v_pad, means.astype(k.dtype)


def moba_attend(q, q_pos, k_pad, v_pad, means, slopes):
    q_len = q.shape[0]
    nb = means.shape[0]
    kv_head = jnp.arange(MOBA_HEADS) // (MOBA_HEADS // MOBA_KV_HEADS)
    gate = jnp.einsum('qhd,nhd->qhn', q, means[:, kv_head]).astype(jnp.float32)
    own = q_pos // MOBA_BLOCK
    gate = jnp.where(jnp.arange(nb)[None, None, :] < own[:, None, None], gate, NEG_INF)
    n_sel = min(MOBA_TOPK, nb)
    _, top = lax.top_k(gate, n_sel)
    top_ok = jnp.broadcast_to(jnp.arange(n_sel)[None, None, :] < own[:, None, None], (q_len, MOBA_HEADS, n_sel))
    blocks = jnp.concatenate([top.astype(jnp.int32), jnp.broadcast_to(own[:, None, None], (q_len, MOBA_HEADS, 1)).astype(jnp.int32)], axis=-1)
    ok = jnp.concatenate([top_ok, jnp.ones((q_len, MOBA_HEADS, 1), dtype=bool)], axis=-1)
    k_pos = (blocks[..., None] * MOBA_BLOCK + jnp.arange(MOBA_BLOCK, dtype=jnp.int32)).reshape(q_len, MOBA_HEADS, -1)
    ok = jnp.repeat(ok, MOBA_BLOCK, axis=-1) & (k_pos <= q_pos[:, None, None])
    kg = k_pad[k_pos, kv_head[None, :, None]]
    vg = v_pad[k_pos, kv_head[None, :, None]]
    dist = (q_pos[:, None, None] - k_pos).astype(jnp.float32)
    s = jnp.einsum('qhd,qhnd->qhn', q, kg).astype(jnp.float32) * (HEAD_DIM ** -0.5) - slopes[None, :, None] * dist
    p = jax.nn.softmax(jnp.where(ok, s, NEG_INF), axis=-1).astype(vg.dtype)
    return jnp.einsum('qhn,qhnd->qhd', p, vg)


def prompt_attention(q_abs, q_rope, ckv, kpe, mq, mk, mv, pos, slopes):
    b, s = q_abs.shape[:2]

    def blocks(a, size):
        return a.reshape((b, s // size, size) + a.shape[2:]).swapaxes(0, 1)

    def unblocks(a):
        a = a.swapaxes(0, 1)
        return a.reshape((b, s) + a.shape[3:])

    lat = lax.map(lambda a: latent_attend(a[0], a[1], a[2], ckv, kpe, pos),
                  (blocks(q_abs, MLA_Q_BLOCK), blocks(q_rope, MLA_Q_BLOCK), pos.reshape(-1, MLA_Q_BLOCK)))
    k_pad, v_pad, means = jax.vmap(moba_prepare)(mk, mv)
    moba_batched = jax.vmap(moba_attend, in_axes=(0, None, 0, 0, 0, None))
    mo = lax.map(lambda a: moba_batched(a[0], a[1], k_pad, v_pad, means, slopes),
                 (blocks(mq, MOBA_Q_CHUNK), pos.reshape(-1, MOBA_Q_CHUNK)))
    return unblocks(lat), unblocks(mo)


def sample_attention(q_abs, q_rope, ckv, kpe, mq, mk, mv, pos, slopes, page_table, cache_ckv, cache_kpe, cache_k, cache_v, layer):
    past = page_table.shape[1] * cache_ckv.shape[2]
    k_pos = jnp.arange(past + pos.shape[0], dtype=jnp.int32)

    def per_sequence(args):
        pages, qa, qr, c_new, r_new, q_m, k_new, v_new = args
        c_all = jnp.concatenate([cache_ckv[layer, pages].reshape(past, KV_LORA), c_new], axis=0)
        r_all = jnp.concatenate([cache_kpe[layer, pages].reshape(past, ROPE_DIM), r_new], axis=0)
        k_all = jnp.concatenate([cache_k[layer, pages].reshape(past, MOBA_KV_HEADS, HEAD_DIM), k_new], axis=0)
        v_all = jnp.concatenate([cache_v[layer, pages].reshape(past, MOBA_KV_HEADS, HEAD_DIM), v_new], axis=0)
        lat = latent_attend(qa[None], qr[None], pos, c_all[None], r_all[None], k_pos)[0]
        k_pad, v_pad, means = moba_prepare(k_all, v_all)
        return lat, moba_attend(q_m, pos, k_pad, v_pad, means, slopes)

    return lax.map(per_sequence, (page_table, q_abs, q_rope, ckv, kpe, mq, mk, mv))


def merge_branches(lat, mo, ga, gb, w_uv, w_oa, w_ob, w_out):
    b, s = lat.shape[:2]
    a = jnp.einsum('bshc,chd->bshd', lat, w_uv).reshape(b, s, MLA_HEADS * V_DIM) @ w_oa
    m = mo.reshape(b, s, MOBA_HEADS * HEAD_DIM) @ w_ob
    return (jax.nn.sigmoid(ga) * a + jax.nn.sigmoid(gb) * m) @ w_out


def moe_ffn(h, w_router, b_router, w_gate_up, b_gate_up, w_down, b_down):
    b, s, d = h.shape
    x = h.reshape(b * s, d)
    m = x.shape[0]
    logits = (x @ w_router + b_router).astype(jnp.float32)
    top_logit, top_e = lax.top_k(logits, TOP_K)
    weight = jax.nn.softmax(top_logit, axis=-1)
    n_pairs = m * TOP_K
    flat_e = top_e.reshape(-1).astype(jnp.int32)
    flat_tok = jnp.repeat(jnp.arange(m, dtype=jnp.int32), TOP_K)
    flat_w = weight.reshape(-1)
    order = jnp.argsort(flat_e)
    e_sorted = flat_e[order]
    counts = jnp.bincount(flat_e, length=N_EXPERTS)
    starts = jnp.cumsum(counts) - counts
    padded = (counts + MOE_ROW_BLOCK - 1) // MOE_ROW_BLOCK * MOE_ROW_BLOCK
    padded_end = jnp.cumsum(padded)
    padded_start = padded_end - padded
    n_blocks = -(-n_pairs // MOE_ROW_BLOCK) + N_EXPERTS
    rows = n_blocks * MOE_ROW_BLOCK
    dest = padded_start[e_sorted] + (jnp.arange(n_pairs, dtype=jnp.int32) - starts[e_sorted])
    row_tok = jnp.full((rows,), m, dtype=jnp.int32).at[dest].set(flat_tok[order])
    row_w = jnp.zeros((rows,), dtype=jnp.float32).at[dest].set(flat_w[order])
    block_e = jnp.minimum(jnp.searchsorted(padded_end, jnp.arange(n_blocks) * MOE_ROW_BLOCK, side='right'), N_EXPERTS - 1).astype(jnp.int32)
    x_ext = jnp.concatenate([x, jnp.zeros((1, d), x.dtype)], axis=0)
    xb = x_ext[row_tok].reshape(n_blocks, MOE_ROW_BLOCK, d)

    def expert_block(args):
        xe, e = args
        gu = xe @ w_gate_up[e] + b_gate_up[e]
        gate = jnp.minimum(gu[:, :D_FF], SWIGLU_LIMIT)
        up = jnp.clip(gu[:, D_FF:], -SWIGLU_LIMIT, SWIGLU_LIMIT)
        return ((up + 1.0) * gate * jax.nn.sigmoid(SWIGLU_ALPHA * gate)) @ w_down[e] + b_down[e]

    yb = lax.map(expert_block, (xb, block_e)).reshape(rows, d)
    y = jnp.zeros((m + 1, d), dtype=yb.dtype).at[row_tok].add(yb * row_w[:, None].astype(yb.dtype))[:m]
    return y.reshape(b, s, d).astype(h.dtype)


def setup_inputs(seed: int = 0) -> dict:
    key = jax.random.key(seed)
    ks = jax.random.split(key, 26)
    f32 = jnp.float32

    def nrm(k, shape, scale=1.0):
        return jax.random.normal(k, shape, f32) * scale

    def gain(k, shape):
        return 1.0 + 0.05 * jax.random.normal(k, shape, f32)

    n_pages = PAST_LEN // PAGE_SIZE
    n_used = DEC_BATCH * n_pages
    n_phys = n_used + max(1, n_used // 4)
    page_table = jax.random.permutation(ks[6], n_phys)[:n_used].reshape(DEC_BATCH, n_pages).astype(jnp.int32)
    L, D = DEPTH, D_MODEL
    return {
        'x_prompt': nrm(ks[0], (BATCH, SEQ, D)),
        'x_sample': nrm(ks[1], (DEC_BATCH, DEC_SEQ, D)),
        'cache_ckv': nrm(ks[2], (L, n_phys, PAGE_SIZE, KV_LORA)),
        'cache_kpe': nrm(ks[3], (L, n_phys, PAGE_SIZE, ROPE_DIM)),
        'cache_k': nrm(ks[4], (L, n_phys, PAGE_SIZE, MOBA_KV_HEADS, HEAD_DIM)),
        'cache_v': nrm(ks[5], (L, n_phys, PAGE_SIZE, MOBA_KV_HEADS, HEAD_DIM)),
        'page_table': page_table,
        'g_attn': gain(ks[7], (L, D)),
        'w_in': nrm(ks[8], (L, D, IN_WIDTH), D ** -0.5),
        'g_q': gain(ks[9], (L, Q_LORA)),
        'w_q_up': nrm(ks[10], (L, Q_LORA, MLA_HEADS * (NOPE_DIM + ROPE_DIM)), Q_LORA ** -0.5),
        'g_kv': gain(ks[11], (L, KV_LORA)),
        'w_uk': nrm(ks[12], (L, KV_LORA, MLA_HEADS, NOPE_DIM), KV_LORA ** -0.5),
        'w_uv': nrm(ks[13], (L, KV_LORA, MLA_HEADS, V_DIM), KV_LORA ** -0.5),
        'w_oa': nrm(ks[14], (L, MLA_HEADS * V_DIM, D), (MLA_HEADS * V_DIM) ** -0.5),
        'w_ob': nrm(ks[15], (L, MOBA_HEADS * HEAD_DIM, D), (MOBA_HEADS * HEAD_DIM) ** -0.5),
        'w_out': nrm(ks[16], (L, D, D), D ** -0.5),
        'g_ffn': gain(ks[17], (L, D)),
        'w_router': nrm(ks[18], (L, D, N_EXPERTS), D ** -0.5),
        'b_router': nrm(ks[19], (L, N_EXPERTS), 0.01),
        'w_gate_up': nrm(ks[20], (L, N_EXPERTS, D, 2 * D_FF), D ** -0.5),
        'b_gate_up': nrm(ks[21], (L, N_EXPERTS, 2 * D_FF), 0.02),
        'w_down': nrm(ks[22], (L, N_EXPERTS, D_FF, D), D_FF ** -0.5),
        'b_down': nrm(ks[23], (L, N_EXPERTS, D), 0.02),
        'g_final': gain(ks[24], (D,)),
    }


def reference(x_prompt, x_sample, cache_ckv, cache_kpe, cache_k, cache_v, page_table,
              g_attn, w_in, g_q, w_q_up, g_kv, w_uk, w_uv, w_oa, w_ob, w_out,
              g_ffn, w_router, b_router, w_gate_up, b_gate_up, w_down, b_down, g_final):
    slopes = alibi_slopes()
    past = page_table.shape[1] * cache_ckv.shape[2]
    pos_p = jnp.arange(x_prompt.shape[1], dtype=jnp.int32)
    pos_s = past + jnp.arange(x_sample.shape[1], dtype=jnp.int32)
    xp, xs = x_prompt, x_sample
    ckv_p, kpe_p, k_p, v_p = [], [], [], []
    ckv_s, kpe_s, k_s, v_s = [], [], [], []
    for l in range(DEPTH):
        proj_w = (w_in[l], g_q[l], w_q_up[l], g_kv[l], w_uk[l])
        merge_w = (w_uv[l], w_oa[l], w_ob[l], w_out[l])
        moe_w = (w_router[l], b_router[l], w_gate_up[l], b_gate_up[l], w_down[l], b_down[l])
        qa, qr, c, r, mq, mk, mv, ga, gb = project_inputs(rmsnorm(xp, g_attn[l]), pos_p, *proj_w)
        lat, mo = prompt_attention(qa, qr, c, r, mq, mk, mv, pos_p, slopes)
        xp = xp + merge_branches(lat, mo, ga, gb, *merge_w)
        xp = xp + moe_ffn(rmsnorm(xp, g_ffn[l]), *moe_w)
        ckv_p.append(c)
        kpe_p.append(r)
        k_p.append(mk)
        v_p.append(mv)
        qa, qr, c, r, mq, mk, mv, ga, gb = project_inputs(rmsnorm(xs, g_attn[l]), pos_s, *proj_w)
        lat, mo = sample_attention(qa, qr, c, r, mq, mk, mv, pos_s, slopes, page_table, cache_ckv, cache_kpe, cache_k, cache_v, l)
        xs = xs + merge_branches(lat, mo, ga, gb, *merge_w)
        xs = xs + moe_ffn(rmsnorm(xs, g_ffn[l]), *moe_w)
        ckv_s.append(c)
        kpe_s.append(r)
        k_s.append(mk)
        v_s.append(mv)
    return (rmsnorm(xp, g_final), rmsnorm(xs, g_final), jnp.stack(ckv_p), jnp.stack(kpe_p), jnp.stack(k_p), jnp.stack(v_p), jnp.stack(ckv_s), jnp.stack(kpe_s), jnp.stack(k_s), jnp.stack(v_s))
```

```python
import functools

import jax
import jax.numpy as jnp
from jax import lax
from jax.experimental import pallas as pl
from jax.experimental.pallas import tpu as pltpu

F32 = jnp.float32
BF16 = jnp.bfloat16

NORM_EPS = 1e-6
ROPE_THETA = 10000.0
MOBA_BLOCK = 256
MOBA_TOPK = 3
MOE_TOP_K = 4
SWIGLU_LIMIT = 7.0
SWIGLU_ALPHA = 1.702
NEG_INF = -1e30

LANES = 128
VMEM_LIMIT = 56 * 1024 * 1024

TOKEN_TILE = 256
MLA_Q_TOKENS = 128
MLA_K_TILE = 512
DECODE_PAGES = 16
EXPERT_ROWS = 512
EXPERT_FF = 256
GATHER_CHUNK = 2048
COMBINE_TOKENS = 128


def _rms(x, g):
    return x * lax.rsqrt(jnp.mean(x * x, axis=-1, keepdims=True) + NORM_EPS) * g


def _dot(a, b):
    return jnp.dot(a, b, preferred_element_type=F32)


def _dot_nt(a, b):
    return lax.dot_general(a, b, (((1,), (1,)), ((), ())), preferred_element_type=F32)


def _topk_onehot(vals, k, valid):
    n = vals.shape[-1]
    iota = lax.broadcasted_iota(jnp.int32, vals.shape, vals.ndim - 1).astype(F32)
    work = jnp.where(valid, vals, -jnp.inf)
    sel = jnp.zeros(vals.shape, F32)
    for _ in range(k):
        m = jnp.max(work, axis=-1, keepdims=True)
        idx = jnp.min(jnp.where(work == m, iota, float(n)), axis=-1, keepdims=True)
        hit = iota == idx
        sel = jnp.where(hit, 1.0, sel)
        work = jnp.where(hit, -jnp.inf, work)
    return jnp.where(valid, sel, 0.0)


def _params(sem, vmem=VMEM_LIMIT):
    return pltpu.CompilerParams(dimension_semantics=sem, vmem_limit_bytes=vmem)


def _const_spec(shape):
    nd = len(shape)
    return pl.BlockSpec(shape, lambda *_: (0,) * nd, pipeline_mode=pl.Buffered(1))


def _proj_body(x_ref, cos_ref, sin_ref, gat_ref, wa_ref, gq_ref, wq_ref, gkv_ref, wuk_ref,
               qabs_ref, qrope_ref, ckv32_ref, ckv16_ref, kpe32_ref, kpe16_ref,
               mq_ref, mk32_ref, mv32_ref, mk16_ref, mv16_ref, kmean_ref,
               *, q_lora, kv_lora, rope, n_mq, n_kv, heads, nope, scale):
    hb = _rms(x_ref[...], gat_ref[...]).astype(BF16)
    off = [0]

    def proj(n):
        r = _dot(hb, wa_ref[:, off[0]:off[0] + n])
        off[0] += n
        return r

    q_lat = proj(q_lora)
    c_raw = proj(kv_lora)
    mq = proj(n_mq)
    mk = proj(n_kv)
    mv = proj(n_kv)
    kpe_a = proj(LANES)[:, :rope]
    kpe_b = proj(LANES)[:, :rope]

    cos = cos_ref[...]
    sin = sin_ref[...]
    ckv = _rms(c_raw, gkv_ref[...])
    ckv32_ref[...] = ckv
    ckv16_ref[...] = ckv.astype(BF16)
    kpe = kpe_a * cos[:, :rope] + kpe_b * sin[:, :rope]
    kpe32_ref[...] = kpe
    kpe16_ref[...] = kpe.astype(BF16)
    mq_ref[...] = mq.astype(BF16)
    mk32_ref[...] = mk
    mv32_ref[...] = mv
    mk16_ref[...] = mk.astype(BF16)
    mv16_ref[...] = mv.astype(BF16)
    kmean_ref[...] = jnp.sum(mk, axis=0, keepdims=True) * (1.0 / MOBA_BLOCK)

    qn = _rms(q_lat, gq_ref[...]).astype(BF16)
    hn = heads * nope
    hr = heads * rope
    q_nope = _dot(qn, wq_ref[:, :hn])
    q_ra = _dot(qn, wq_ref[:, hn:hn + hr])
    q_rb = _dot(qn, wq_ref[:, hn + hr:hn + 2 * hr])
    qrope_ref[...] = ((q_ra * cos + q_rb * sin) * scale).astype(BF16)
    for h in range(heads):
        qa = _dot(q_nope[:, h * nope:(h + 1) * nope].astype(BF16), wuk_ref[h])
        qabs_ref[:, h * kv_lora:(h + 1) * kv_lora] = (qa * scale).astype(BF16)


def _proj_call(x, cos_t, sin_t, g_attn, wa, g_q, wq, g_kv, wuk, dims):
    t, d = x.shape
    tm = TOKEN_TILE
    heads, nope, rope = dims["heads"], dims["nope"], dims["rope"]
    q_lora, kv_lora, n_mq, n_kv = dims["q_lora"], dims["kv_lora"], dims["n_mq"], dims["n_kv"]
    nt = t // tm
    row = lambda w: pl.BlockSpec((tm, w), lambda i: (i, 0))
    out_shape = (
        jax.ShapeDtypeStruct((t, heads * kv_lora), BF16),
        jax.ShapeDtypeStruct((t, heads * rope), BF16),
        jax.ShapeDtypeStruct((t, kv_lora), F32),
        jax.ShapeDtypeStruct((t, kv_lora), BF16),
        jax.ShapeDtypeStruct((t, rope), F32),
        jax.ShapeDtypeStruct((t, rope), BF16),
        jax.ShapeDtypeStruct((t, n_mq), BF16),
        jax.ShapeDtypeStruct((t, n_kv), F32),
        jax.ShapeDtypeStruct((t, n_kv), F32),
        jax.ShapeDtypeStruct((t, n_kv), BF16),
        jax.ShapeDtypeStruct((t, n_kv), BF16),
        jax.ShapeDtypeStruct((nt, 1, n_kv), F32),
    )
    out_specs = (row(heads * kv_lora), row(heads * rope), row(kv_lora), row(kv_lora), row(rope), row(rope),
                 row(n_mq), row(n_kv), row(n_kv), row(n_kv), row(n_kv),
                 pl.BlockSpec((None, 1, n_kv), lambda i: (i, 0, 0)))
    in_specs = [row(d), row(heads * rope), row(heads * rope), _const_spec(g_attn.shape), _const_spec(wa.shape),
                _const_spec(g_q.shape), _const_spec(wq.shape), _const_spec(g_kv.shape), _const_spec(wuk.shape)]
    body = functools.partial(_proj_body, q_lora=q_lora, kv_lora=kv_lora, rope=rope, n_mq=n_mq, n_kv=n_kv,
                             heads=heads, nope=nope, scale=dims["mla_scale"])
    return pl.pallas_call(body, grid=(nt,), in_specs=in_specs, out_specs=out_specs, out_shape=out_shape,
                          compiler_params=_params(("arbitrary",)), name="proj")(
                              x, cos_t, sin_t, g_attn, wa, g_q, wq, g_kv, wuk)


def _mla_prompt_body(qa_ref, qr_ref, ckv_ref, kpe_ref, out_ref, m_scr, l_scr, acc_scr, *, tq, tk, heads):
    qi = pl.program_id(1)
    rows = tq * heads
    qa = qa_ref[...]
    qr = qr_ref[...]
    m_scr[...] = jnp.full(m_scr.shape, NEG_INF, F32)
    l_scr[...] = jnp.zeros(l_scr.shape, F32)
    acc_scr[...] = jnp.zeros(acc_scr.shape, F32)
    tok = qi * tq + lax.div(lax.broadcasted_iota(jnp.int32, (rows, 1), 0), heads)
    n_k = lax.div(qi * tq, tk) + 1

    def step(kt, carry):
        start = pl.multiple_of(kt * tk, tk)
        kc = ckv_ref[pl.ds(start, tk), :]
        kr = kpe_ref[pl.ds(start, tk), :]
        s = _dot_nt(qa, kc) + _dot_nt(qr, kr)
        key = start + lax.broadcasted_iota(jnp.int32, (1, tk), 1)
        s = jnp.where(key <= tok, s, NEG_INF)
        m_old = m_scr[...]
        m_new = jnp.maximum(m_old, jnp.max(s, axis=-1, keepdims=True))
        alpha = jnp.exp(m_old - m_new)
        p = jnp.exp(s - m_new)
        l_scr[...] = alpha * l_scr[...] + jnp.sum(p, axis=-1, keepdims=True)
        acc_scr[...] = alpha * acc_scr[...] + _dot(p.astype(BF16), kc)
        m_scr[...] = m_new
        return carry

    lax.fori_loop(0, n_k, step, 0)
    out_ref[...] = (acc_scr[...] / l_scr[...]).astype(BF16)


def _mla_prompt_call(qabs, qrope, ckv16, kpe16, batch, seq, dims):
    t = ckv16.shape[0]
    heads, kv_lora, rope = dims["heads"], dims["kv_lora"], dims["rope"]
    tq = min(MLA_Q_TOKENS, seq)
    tk = min(MLA_K_TILE, seq)
    nq = seq // tq
    rows = tq * heads
    qa2 = qabs.reshape(t * heads, kv_lora)
    qr2 = qrope.reshape(t * heads, rope)
    body = functools.partial(_mla_prompt_body, tq=tq, tk=tk, heads=heads)
    return pl.pallas_call(
        body, grid=(batch, nq),
        in_specs=[pl.BlockSpec((rows, kv_lora), lambda b, q: (b * nq + q, 0)),
                  pl.BlockSpec((rows, rope), lambda b, q: (b * nq + q, 0)),
                  pl.BlockSpec((seq, kv_lora), lambda b, q: (b, 0)),
                  pl.BlockSpec((seq, rope), lambda b, q: (b, 0))],
        out_specs=pl.BlockSpec((rows, kv_lora), lambda b, q: (b * nq + q, 0)),
        out_shape=jax.ShapeDtypeStruct((batch * seq * heads, kv_lora), BF16),
        scratch_shapes=[pltpu.VMEM((rows, 1), F32), pltpu.VMEM((rows, 1), F32), pltpu.VMEM((rows, kv_lora), F32)],
        compiler_params=_params(("arbitrary", "arbitrary")), name="mla_prompt")(qa2, qr2, ckv16, kpe16)


def _moba_prompt_body(q_ref, k_ref, v_ref, kmean_ref, slope_ref, out_ref, m_scr, l_scr, acc_scr,
                      *, gh, dh, nb, scale):
    qi = pl.program_id(2)
    blk = MOBA_BLOCK
    rows = gh * blk
    q = jnp.concatenate([q_ref[:, j * dh:(j + 1) * dh] for j in range(gh)], axis=0)
    slope = slope_ref[...]
    tpos = lax.broadcasted_iota(jnp.int32, (blk, 1), 0)
    qpos = qi * blk + jnp.concatenate([tpos] * gh, axis=0)
    qposf = qpos.astype(F32)

    gate = _dot_nt(q, kmean_ref[...].astype(BF16))
    biota = lax.broadcasted_iota(jnp.int32, (rows, nb), 1)
    sel = _topk_onehot(gate, MOBA_TOPK, biota < qi)

    def scores(kt):
        start = pl.multiple_of(kt * blk, blk)
        kb = k_ref[pl.ds(start, blk), :]
        vb = v_ref[pl.ds(start, blk), :]
        kpos = start + lax.broadcasted_iota(jnp.int32, (1, blk), 1)
        s = _dot_nt(q, kb) * scale - slope * (qposf - kpos.astype(F32))
        return s, vb, kpos

    s, vb, kpos = scores(qi)
    s = jnp.where(kpos <= qpos, s, NEG_INF)
    m0 = jnp.max(s, axis=-1, keepdims=True)
    p = jnp.exp(s - m0)
    m_scr[...] = m0
    l_scr[...] = jnp.sum(p, axis=-1, keepdims=True)
    acc_scr[...] = _dot(p.astype(BF16), vb)

    def step(kt, carry):
        s, vb, _ = scores(kt)
        chosen = jnp.sum(jnp.where(biota == kt, sel, 0.0), axis=-1, keepdims=True)
        s = jnp.where(chosen > 0.0, s, NEG_INF)
        m_old = m_scr[...]
        m_new = jnp.maximum(m_old, jnp.max(s, axis=-1, keepdims=True))
        alpha = jnp.exp(m_old - m_new)
        p = jnp.exp(s - m_new)
        l_scr[...] = alpha * l_scr[...] + jnp.sum(p, axis=-1, keepdims=True)
        acc_scr[...] = alpha * acc_scr[...] + _dot(p.astype(BF16), vb)
        m_scr[...] = m_new
        return carry

    lax.fori_loop(0, qi, step, 0)
    o = acc_scr[...] / l_scr[...]
    for j in range(gh):
        out_ref[:, j * dh:(j + 1) * dh] = o[j * blk:(j + 1) * blk, :].astype(BF16)


def _moba_prompt_call(mq16, mk16, mv16, kmean, slope_rows, batch, seq, dims):
    t = mq16.shape[0]
    heads, kvh, dh = dims["m_heads"], dims["m_kvh"], dims["dh"]
    gh = heads // kvh
    blk = MOBA_BLOCK
    nb = seq // blk
    rows = gh * blk
    body = functools.partial(_moba_prompt_body, gh=gh, dh=dh, nb=nb, scale=dh ** -0.5)
    return pl.pallas_call(
        body, grid=(batch, kvh, nb),
        in_specs=[pl.BlockSpec((blk, gh * dh), lambda b, g, q: (b * nb + q, g)),
                  pl.BlockSpec((seq, dh), lambda b, g, q: (b, g)),
                  pl.BlockSpec((seq, dh), lambda b, g, q: (b, g)),
                  pl.BlockSpec((nb, dh), lambda b, g, q: (b, g)),
                  pl.BlockSpec((None, rows, 1), lambda b, g, q: (g, 0, 0))],
        out_specs=pl.BlockSpec((blk, gh * dh), lambda b, g, q: (b * nb + q, g)),
        out_shape=jax.ShapeDtypeStruct((batch * seq, heads * dh), BF16),
        scratch_shapes=[pltpu.VMEM((rows, 1), F32), pltpu.VMEM((rows, 1), F32), pltpu.VMEM((rows, dh), F32)],
        compiler_params=_params(("arbitrary", "arbitrary", "arbitrary")), name="moba_prompt")(
            mq16, mk16, mv16, kmean, slope_rows)


def _page_spec(block, layer, pages_per_step, i, n_steps=None, phase=0):
    nd = len(block)

    def index(s, j, pt):
        jj = j - phase
        if n_steps is not None:
            jj = jnp.clip(jj, 0, n_steps - 1)
        return (layer, pt[s, jj * pages_per_step + i]) + (0,) * (nd - 2)

    return pl.BlockSpec(block, index)


def _mla_decode_body(pt_ref, qa_ref, qr_ref, cnew_ref, rnew_ref, *rest, pages, heads, n_new):
    ckv_pages = rest[:pages]
    kpe_pages = rest[pages:2 * pages]
    out_ref, m_scr, l_scr, acc_scr = rest[2 * pages:]
    j = pl.program_id(1)
    qa = qa_ref[...]
    qr = qr_ref[...]
    rows = qa.shape[0]

    @pl.when(j == 0)
    def _():
        cn = cnew_ref[...]
        s = _dot_nt(qa, cn) + _dot_nt(qr, rnew_ref[...])
        tok = lax.div(lax.broadcasted_iota(jnp.int32, (rows, 1), 0), heads)
        key = lax.broadcasted_iota(jnp.int32, (1, cn.shape[0]), 1)
        s = jnp.where((key <= tok) & (key < n_new), s, NEG_INF)
        m0 = jnp.max(s, axis=-1, keepdims=True)
        p = jnp.exp(s - m0)
        m_scr[...] = m0
        l_scr[...] = jnp.sum(p, axis=-1, keepdims=True)
        acc_scr[...] = _dot(p.astype(BF16), cn)

    kc = jnp.concatenate([r[...].astype(BF16) for r in ckv_pages], axis=0)
    kr = jnp.concatenate([r[...].astype(BF16) for r in kpe_pages], axis=0)
    s = _dot_nt(qa, kc) + _dot_nt(qr, kr)
    m_old = m_scr[...]
    m_new = jnp.maximum(m_old, jnp.max(s, axis=-1, keepdims=True))
    alpha = jnp.exp(m_old - m_new)
    p = jnp.exp(s - m_new)
    l_scr[...] = alpha * l_scr[...] + jnp.sum(p, axis=-1, keepdims=True)
    acc_scr[...] = alpha * acc_scr[...] + _dot(p.astype(BF16), kc)
    m_scr[...] = m_new

    @pl.when(j == pl.num_programs(1) - 1)
    def _():
        out_ref[...] = acc_scr[...] / l_scr[...]


def _mla_decode_call(page_table, qa_s, qr_s, cnew, rnew, cache_ckv, cache_kpe, layer, dims, n_new):
    ns, rows, kv_lora = qa_s.shape
    rope = qr_s.shape[-1]
    n_pages = page_table.shape[1]
    page = cache_ckv.shape[2]
    pages = min(DECODE_PAGES, n_pages)
    nj = n_pages // pages
    npad = cnew.shape[1]
    seq_spec = lambda r, w: pl.BlockSpec((None, r, w), lambda s, j, pt: (s, 0, 0))
    in_specs = [seq_spec(rows, kv_lora), seq_spec(rows, rope), seq_spec(npad, kv_lora), seq_spec(npad, rope)]
    in_specs += [_page_spec((None, None, page, kv_lora), layer, pages, i) for i in range(pages)]
    in_specs += [_page_spec((None, None, page, rope), layer, pages, i) for i in range(pages)]
    body = functools.partial(_mla_decode_body, pages=pages, heads=dims["heads"], n_new=n_new)
    grid_spec = pltpu.PrefetchScalarGridSpec(
        num_scalar_prefetch=1, grid=(ns, nj), in_specs=in_specs, out_specs=seq_spec(rows, kv_lora),
        scratch_shapes=[pltpu.VMEM((rows, 1), F32), pltpu.VMEM((rows, 1), F32), pltpu.VMEM((rows, kv_lora), F32)])
    return pl.pallas_call(body, grid_spec=grid_spec, out_shape=jax.ShapeDtypeStruct((ns, rows, kv_lora), F32),
                          compiler_params=_params(("arbitrary", "arbitrary")), name="mla_decode")(
                              page_table, qa_s, qr_s, cnew, rnew, *([cache_ckv] * pages), *([cache_kpe] * pages))


def _moba_decode_body(pt_ref, q_ref, knew_ref, vnew_ref, slope_ref, *rest,
                      pages, page, kvh, dh, n_new, past, scale):
    k_pages = rest[:pages]
    v_pages = rest[pages:2 * pages]
    out_ref, s_scr, mean_scr, sel_scr, m_scr, l_scr, acc_scr = rest[2 * pages:]
    j = pl.program_id(1)
    nj = pl.num_programs(1) // 2
    blk = MOBA_BLOCK
    keys = pages * page
    bps = keys // blk
    nblk = mean_scr.shape[1]
    rows = q_ref.shape[1]
    tok = lax.rem(lax.broadcasted_iota(jnp.int32, (rows, 1), 0), n_new)
    qposf = (past + tok).astype(F32)

    @pl.when(j < nj)
    def _():
        k_all = jnp.concatenate([r[...] for r in k_pages], axis=0)
        for g in range(kvh):
            kg = k_all[:, g * dh:(g + 1) * dh]
            sums = jnp.sum(kg.reshape(bps, blk, dh), axis=1)
            mean_scr[g, pl.ds(pl.multiple_of(j * bps, bps), bps), :] = sums * (1.0 / blk)
            s_scr[g, j] = _dot_nt(q_ref[g], kg.astype(BF16))

    @pl.when(j == nj)
    def _():
        for g in range(kvh):
            q = q_ref[g]
            gate = _dot_nt(q, mean_scr[g].astype(BF16))
            sel_scr[g] = _topk_onehot(gate, MOBA_TOPK, jnp.full(gate.shape, True))
            kn = knew_ref[:, g * dh:(g + 1) * dh]
            key = lax.broadcasted_iota(jnp.int32, (1, kn.shape[0]), 1)
            s = _dot_nt(q, kn) * scale - slope_ref[g] * (tok - key).astype(F32)
            s = jnp.where((key <= tok) & (key < n_new), s, NEG_INF)
            m0 = jnp.max(s, axis=-1, keepdims=True)
            p = jnp.exp(s - m0)
            m_scr[g] = m0
            l_scr[g] = jnp.sum(p, axis=-1, keepdims=True)
            acc_scr[g] = _dot(p.astype(BF16), vnew_ref[:, g * dh:(g + 1) * dh])

    @pl.when(j >= nj)
    def _():
        jj = j - nj
        v_all = jnp.concatenate([r[...].astype(BF16) for r in v_pages], axis=0)
        kpos = jj * keys + lax.broadcasted_iota(jnp.int32, (1, keys), 1)
        lo = (lax.broadcasted_iota(jnp.int32, (nblk, keys), 0) - jj * bps) * blk
        ci = lax.broadcasted_iota(jnp.int32, (nblk, keys), 1)
        expand = jnp.where((ci >= lo) & (ci < lo + blk), 1.0, 0.0).astype(BF16)
        for g in range(kvh):
            chosen = _dot(sel_scr[g].astype(BF16), expand)
            s = s_scr[g, jj] * scale - slope_ref[g] * (qposf - kpos.astype(F32))
            s = jnp.where(chosen > 0.5, s, NEG_INF)
            m_old = m_scr[g]
            m_new = jnp.maximum(m_old, jnp.max(s, axis=-1, keepdims=True))
            alpha = jnp.exp(m_old - m_new)
            p = jnp.exp(s - m_new)
            l_scr[g] = alpha * l_scr[g] + jnp.sum(p, axis=-1, keepdims=True)
            acc_scr[g] = alpha * acc_scr[g] + _dot(p.astype(BF16), v_all[:, g * dh:(g + 1) * dh])
            m_scr[g] = m_new

    @pl.when(j == 2 * nj - 1)
    def _():
        for g in range(kvh):
            out_ref[g] = acc_scr[g] / l_scr[g]


def _moba_decode_call(page_table, q_s, knew, vnew, slope_rows, cache_k, cache_v, layer, n_new):
    ns, kvh, rows, dh = q_s.shape
    n_pages = page_table.shape[1]
    page = cache_k.shape[2]
    width = cache_k.shape[3]
    pages = min(DECODE_PAGES, n_pages)
    nj = n_pages // pages
    past = n_pages * page
    nblk = past // MOBA_BLOCK
    keys = pages * page
    npad = knew.shape[1]
    in_specs = [pl.BlockSpec((None, kvh, rows, dh), lambda s, j, pt: (s, 0, 0, 0)),
                pl.BlockSpec((None, npad, width), lambda s, j, pt: (s, 0, 0)),
                pl.BlockSpec((None, npad, width), lambda s, j, pt: (s, 0, 0)),
                pl.BlockSpec((kvh, rows, 1), lambda s, j, pt: (0, 0, 0))]
    in_specs += [_page_spec((None, None, page, width), layer, pages, i, n_steps=nj, phase=0) for i in range(pages)]
    in_specs += [_page_spec((None, None, page, width), layer, pages, i, n_steps=nj, phase=nj) for i in range(pages)]
    body = functools.partial(_moba_decode_body, pages=pages, page=page, kvh=kvh, dh=dh, n_new=n_new, past=past,
                             scale=dh ** -0.5)
    grid_spec = pltpu.PrefetchScalarGridSpec(
        num_scalar_prefetch=1, grid=(ns, 2 * nj), in_specs=in_specs,
        out_specs=pl.BlockSpec((None, kvh, rows, dh), lambda s, j, pt: (s, 0, 0, 0)),
        scratch_shapes=[pltpu.VMEM((kvh, nj, rows, keys), F32), pltpu.VMEM((kvh, nblk, dh), F32),
                        pltpu.VMEM((kvh, rows, nblk), F32), pltpu.VMEM((kvh, rows, 1), F32),
                        pltpu.VMEM((kvh, rows, 1), F32), pltpu.VMEM((kvh, rows, dh), F32)])
    return pl.pallas_call(body, grid_spec=grid_spec, out_shape=jax.ShapeDtypeStruct((ns, kvh, rows, dh), F32),
                          compiler_params=_params(("arbitrary", "arbitrary")), name="moba_decode")(
                              page_table, q_s, knew, vnew, slope_rows, *([cache_k] * pages), *([cache_v] * pages))


def _merge1_body(x_ref, latp_ref, lats_ref, mop_ref, mos_ref, gat_ref, wg_ref, wuv_ref, woa_ref, wob_ref, out_ref,
                 *, heads, kv_lora, prompt_tiles):
    d = x_ref.shape[1]
    is_prompt = pl.program_id(0) < prompt_tiles
    lat = jnp.where(is_prompt, latp_ref[...], lats_ref[...])
    mo = jnp.where(is_prompt, mop_ref[...], mos_ref[...])
    hb = _rms(x_ref[...], gat_ref[...]).astype(BF16)
    ga = _dot(hb, wg_ref[:, :d])
    gb = _dot(hb, wg_ref[:, d:])
    a_in = jnp.concatenate(
        [_dot(lat[:, h * kv_lora:(h + 1) * kv_lora], wuv_ref[h]).astype(BF16) for h in range(heads)], axis=1)
    a = _dot(a_in, woa_ref[...])
    m = _dot(mo, wob_ref[...])
    out_ref[...] = (jax.nn.sigmoid(ga) * a + jax.nn.sigmoid(gb) * m).astype(BF16)


def _merge1_call(x, lat_p, lat_s, mo_p, mo_s, g_attn, wg, wuv, woa, wob, dims):
    t, d = x.shape
    tm = TOKEN_TILE
    npt = lat_p.shape[0] // tm
    row = lambda w: pl.BlockSpec((tm, w), lambda i: (i, 0))
    prow = lambda w: pl.BlockSpec((tm, w), lambda i: (jnp.minimum(i, npt - 1), 0))
    srow = lambda w: pl.BlockSpec((tm, w), lambda i: (jnp.maximum(i - npt, 0), 0))
    body = functools.partial(_merge1_body, heads=dims["heads"], kv_lora=dims["kv_lora"], prompt_tiles=npt)
    return pl.pallas_call(
        body, grid=(t // tm,),
        in_specs=[row(d), prow(lat_p.shape[1]), srow(lat_s.shape[1]), prow(mo_p.shape[1]), srow(mo_s.shape[1]),
                  _const_spec(g_attn.shape), _const_spec(wg.shape),
                  _const_spec(wuv.shape), _const_spec(woa.shape), _const_spec(wob.shape)],
        out_specs=row(d), out_shape=jax.ShapeDtypeStruct((t, d), BF16),
        compiler_params=_params(("arbitrary",)), name="merge1")(
            x, lat_p, lat_s, mo_p, mo_s, g_attn, wg, wuv, woa, wob)


def _merge2_body(gated_ref, x_ref, wout_ref, gffn_ref, wr_hi_ref, wr_lo_ref, br_ref, x2_ref, h2_ref, selw_ref,
                 *, n_experts):
    x2 = x_ref[...] + _dot(gated_ref[...], wout_ref[...])
    x2_ref[...] = x2
    h2 = _rms(x2, gffn_ref[...])
    h2_ref[...] = h2
    h_hi = h2.astype(BF16)
    h_lo = (h2 - h_hi.astype(F32)).astype(BF16)
    logits = _dot(h_hi, wr_hi_ref[...]) + _dot(h_hi, wr_lo_ref[...]) + _dot(h_lo, wr_hi_ref[...]) + br_ref[...]
    valid = lax.broadcasted_iota(jnp.int32, logits.shape, 1) < n_experts
    sel = _topk_onehot(logits, MOE_TOP_K, valid)
    top = jnp.max(jnp.where(valid, logits, -jnp.inf), axis=-1, keepdims=True)
    e = jnp.where(sel > 0.0, jnp.exp(logits - top), 0.0)
    w = e / jnp.sum(e, axis=-1, keepdims=True)
    selw_ref[...] = jnp.where(sel > 0.0, w, -1.0)


def _merge2_call(gated, x, wout, g_ffn, wr_hi, wr_lo, br, n_experts):
    t, d = x.shape
    tm = TOKEN_TILE
    row = lambda w: pl.BlockSpec((tm, w), lambda i: (i, 0))
    body = functools.partial(_merge2_body, n_experts=n_experts)
    return pl.pallas_call(
        body, grid=(t // tm,),
        in_specs=[row(d), row(d), _const_spec(wout.shape), _const_spec(g_ffn.shape), _const_spec(wr_hi.shape),
                  _const_spec(wr_lo.shape), _const_spec(br.shape)],
        out_specs=(row(d), row(d), row(LANES)),
        out_shape=(jax.ShapeDtypeStruct((t, d), F32), jax.ShapeDtypeStruct((t, d), F32),
                   jax.ShapeDtypeStruct((t, LANES), F32)),
        compiler_params=_params(("arbitrary",)), name="merge2")(gated, x, wout, g_ffn, wr_hi, wr_lo, br)


def _row_copy(src_hbm, src_row, dst, dst_row, sem):
    return pltpu.make_async_copy(src_hbm.at[pl.ds(src_row, 1), :], dst.at[pl.ds(dst_row, 1), :], sem)


def _gather_body(rt_ref, h_hbm, xs_hbm, sem, *, chunk):
    base = pl.program_id(0) * chunk

    def issue(r, carry):
        _row_copy(h_hbm, rt_ref[0, r], xs_hbm, base + r, sem).start()
        return carry

    lax.fori_loop(0, chunk, issue, 0)

    def drain(r, carry):
        _row_copy(h_hbm, 0, xs_hbm, base + r, sem).wait()
        return carry

    lax.fori_loop(0, chunk, drain, 0)


def _gather_call(row_tok, h2):
    n_rows = row_tok.shape[0]
    d = h2.shape[1]
    n_tiles = n_rows // EXPERT_ROWS
    chunk = EXPERT_ROWS * max(g for g in (1, 2, 4) if n_tiles % g == 0 and EXPERT_ROWS * g <= GATHER_CHUNK)
    nc = n_rows // chunk
    body = functools.partial(_gather_body, chunk=chunk)
    return pl.pallas_call(
        body, grid=(nc,),
        in_specs=[pl.BlockSpec((None, 1, chunk), lambda c: (c, 0, 0), memory_space=pltpu.SMEM),
                  pl.BlockSpec(memory_space=pl.ANY)],
        out_specs=pl.BlockSpec(memory_space=pl.ANY),
        out_shape=jax.ShapeDtypeStruct((n_rows, d), h2.dtype),
        scratch_shapes=[pltpu.SemaphoreType.DMA(())],
        compiler_params=_params(("arbitrary",)), name="moe_gather")(row_tok.reshape(nc, 1, chunk), h2)


def _expert_body(te_ref, nu_ref, x_ref, wg_ref, wu_ref, bg_ref, bu_ref, wd_ref, bd_ref, out_ref, xb_scr):
    i = pl.program_id(0)
    f = pl.program_id(1)

    @pl.when(i < nu_ref[0])
    def _():
        @pl.when(f == 0)
        def _():
            xb_scr[...] = x_ref[...].astype(BF16)

        xb = xb_scr[...]
        g = _dot(xb, wg_ref[...].astype(BF16)) + bg_ref[...]
        u = _dot(xb, wu_ref[...].astype(BF16)) + bu_ref[...]
        gate = jnp.minimum(g, SWIGLU_LIMIT)
        up = jnp.clip(u, -SWIGLU_LIMIT, SWIGLU_LIMIT)
        act = (up + 1.0) * gate * jax.nn.sigmoid(SWIGLU_ALPHA * gate)
        y = _dot(act.astype(BF16), wd_ref[...].astype(BF16))

        @pl.when(f == 0)
        def _():
            out_ref[...] = y + bd_ref[...]

        @pl.when(f > 0)
        def _():
            out_ref[...] += y


def _expert_call(tile_expert, n_used, xs, moe_w, layer):
    w_gate_up, b_gate_up, w_down, b_down = moe_w
    n_rows, d = xs.shape
    depth, n_exp, _, two_f = w_gate_up.shape
    ff = two_f // 2
    tm = EXPERT_ROWS
    tf = min(EXPERT_FF, ff)
    nf = ff // tf
    nt = n_rows // tm
    bgu = b_gate_up.reshape(depth, n_exp, 1, two_f)
    bd = b_down.reshape(depth, n_exp, 1, d)

    def tile(i, nu):
        return jnp.minimum(i, nu[0] - 1)

    def expert(i, te, nu):
        return te[tile(i, nu)]

    def col(i, f, nu):
        return jnp.where(i < nu[0], f, nf - 1)

    in_specs = [
        pl.BlockSpec((tm, d), lambda i, f, te, nu: (tile(i, nu), 0)),
        pl.BlockSpec((None, None, d, tf), lambda i, f, te, nu: (layer, expert(i, te, nu), 0, col(i, f, nu))),
        pl.BlockSpec((None, None, d, tf), lambda i, f, te, nu: (layer, expert(i, te, nu), 0, nf + col(i, f, nu))),
        pl.BlockSpec((None, None, 1, tf), lambda i, f, te, nu: (layer, expert(i, te, nu), 0, col(i, f, nu))),
        pl.BlockSpec((None, None, 1, tf), lambda i, f, te, nu: (layer, expert(i, te, nu), 0, nf + col(i, f, nu))),
        pl.BlockSpec((None, None, tf, d), lambda i, f, te, nu: (layer, expert(i, te, nu), col(i, f, nu), 0)),
        pl.BlockSpec((None, None, 1, d), lambda i, f, te, nu: (layer, expert(i, te, nu), 0, 0)),
    ]
    grid_spec = pltpu.PrefetchScalarGridSpec(
        num_scalar_prefetch=2, grid=(nt, nf), in_specs=in_specs,
        out_specs=pl.BlockSpec((tm, d), lambda i, f, te, nu: (tile(i, nu), 0)),
        scratch_shapes=[pltpu.VMEM((tm, d), BF16)])
    return pl.pallas_call(
        _expert_body, grid_spec=grid_spec, out_shape=jax.ShapeDtypeStruct((n_rows, d), F32),
        compiler_params=_params(("arbitrary", "arbitrary")), name="moe_experts")(
            tile_expert, n_used, xs, w_gate_up, w_gate_up, bgu, bgu, w_down, bd)


def _combine_body(pos_cur_ref, pos_next_ref, w_ref, x2_ref, gf_ref, yb_hbm, y_ref, buf, sem, *, tm, top_k, final):
    i = pl.program_id(0)
    n = pl.num_programs(0)
    slot = lax.rem(i, 2)

    def issue(pos_ref, s):
        for k in range(top_k):
            def one(r, carry, k=k):
                _row_copy(yb_hbm, pos_ref[0, k * tm + r], buf.at[s, k], r, sem.at[s]).start()
                return carry
            lax.fori_loop(0, tm, one, 0)

    @pl.when(i == 0)
    def _():
        issue(pos_cur_ref, 0)

    @pl.when(i + 1 < n)
    def _():
        issue(pos_next_ref, 1 - slot)

    for k in range(top_k):
        def drain(r, carry, k=k):
            _row_copy(yb_hbm, 0, buf.at[slot, k], r, sem.at[slot]).wait()
            return carry
        lax.fori_loop(0, tm, drain, 0)

    w = w_ref[...]
    moe = w[:, 0:1] * buf[slot, 0]
    for k in range(1, top_k):
        moe = moe + w[:, k:k + 1] * buf[slot, k]
    y = x2_ref[...] + moe
    y_ref[...] = _rms(y, gf_ref[...]) if final else y


def _combine_call(pos4, w4, x2, g_final, yb, final):
    t, d = x2.shape
    top_k = pos4.shape[1]
    tm = COMBINE_TOKENS
    nt = t // tm
    pos_tiles = pos4.reshape(nt, tm, top_k).transpose(0, 2, 1).reshape(nt, 1, top_k * tm)
    body = functools.partial(_combine_body, tm=tm, top_k=top_k, final=final)
    smem = lambda index: pl.BlockSpec((None, 1, top_k * tm), index, memory_space=pltpu.SMEM)
    return pl.pallas_call(
        body, grid=(nt,),
        in_specs=[smem(lambda i: (i, 0, 0)), smem(lambda i: (jnp.minimum(i + 1, nt - 1), 0, 0)),
                  pl.BlockSpec((tm, top_k), lambda i: (i, 0)), pl.BlockSpec((tm, d), lambda i: (i, 0)),
                  _const_spec(g_final.shape), pl.BlockSpec(memory_space=pl.ANY)],
        out_specs=pl.BlockSpec((tm, d), lambda i: (i, 0)),
        out_shape=jax.ShapeDtypeStruct((t, d), F32),
        scratch_shapes=[pltpu.VMEM((2, top_k, tm, d), F32), pltpu.SemaphoreType.DMA((2,))],
        compiler_params=_params(("arbitrary",)), name="moe_combine")(pos_tiles, pos_tiles, w4, x2, g_final, yb)


def _route(selw, n_experts, tile_rows):
    t = selw.shape[0]
    top_w, top_e = lax.top_k(selw, MOE_TOP_K)
    sel = (selw >= 0.0).astype(jnp.int32)
    counts = jnp.sum(sel, axis=0)
    tiles_per = (counts + tile_rows - 1) // tile_rows
    tile_end = jnp.cumsum(tiles_per)
    n_used = tile_end[-1]
    pstart = (tile_end - tiles_per) * tile_rows
    starts = jnp.cumsum(counts) - counts
    n_pairs = t * MOE_TOP_K
    n_tiles = -(-n_pairs // tile_rows) + n_experts
    tile_expert = jnp.minimum(jnp.searchsorted(tile_end, jnp.arange(n_tiles), side="right"), n_experts - 1)
    tile_expert = tile_expert.astype(jnp.int32)
    flat_e = top_e.reshape(-1).astype(jnp.int32)
    flat_tok = jnp.repeat(jnp.arange(t, dtype=jnp.int32), MOE_TOP_K)
    order = jnp.argsort(flat_e)
    rows = jnp.arange(n_tiles * tile_rows, dtype=jnp.int32)
    e_r = tile_expert[rows // tile_rows]
    j = rows - pstart[e_r]
    valid = (j < counts[e_r]) & (rows < n_used * tile_rows)
    src = jnp.clip(starts[e_r] + j, 0, n_pairs - 1)
    row_tok = jnp.where(valid, flat_tok[order[src]], 0).astype(jnp.int32)
    rank = jnp.cumsum(sel, axis=0) - 1
    pos = pstart[None, :] + rank
    pos4 = jnp.take_along_axis(pos, top_e, axis=1).astype(jnp.int32)
    return tile_expert, n_used.reshape(1).astype(jnp.int32), row_tok, pos4, top_w


def _rope_tables(pos, rope, heads):
    half = rope // 2
    inv = ROPE_THETA ** (-jnp.arange(half, dtype=F32) / half)
    ang = pos.astype(F32)[:, None] * inv[None, :]
    cos, sin = jnp.cos(ang), jnp.sin(ang)
    cos_t = jnp.tile(jnp.concatenate([cos, cos], axis=1), (1, heads))
    sin_t = jnp.tile(jnp.concatenate([-sin, sin], axis=1), (1, heads))
    return cos_t, sin_t


def _swap_halves(w, rope):
    half = rope // 2
    return jnp.concatenate([w[..., half:], w[..., :half]], axis=-1)


def _layer(x, pos, batch, seq, dec_batch, dec_seq, caches, page_table, lw, moe_w, layer, dims):
    cache_ckv, cache_kpe, cache_k, cache_v = caches
    t, d = x.shape
    heads, nope, rope, kv_lora = dims["heads"], dims["nope"], dims["rope"], dims["kv_lora"]
    m_heads, kvh, dh = dims["m_heads"], dims["m_kvh"], dims["dh"]
    tp = batch * seq

    w_in = lw["w_in"]
    sizes = (dims["q_lora"], kv_lora, rope, m_heads * dh, kvh * dh, kvh * dh, d, d)
    offs = [0]
    for n in sizes:
        offs.append(offs[-1] + n)
    col = lambda k: w_in[:, offs[k]:offs[k + 1]]
    pad = jnp.zeros((d, LANES - rope), F32)
    wa = jnp.concatenate([col(0), col(1), col(3), col(4), col(5), col(2), pad, _swap_halves(col(2), rope), pad],
                         axis=1).astype(BF16)
    wg = jnp.concatenate([col(6), col(7)], axis=1).astype(BF16)
    wq3 = lw["w_q_up"].reshape(dims["q_lora"], heads, nope + rope)
    wq_rope = wq3[:, :, nope:]
    wq = jnp.concatenate([wq3[:, :, :nope].reshape(-1, heads * nope), wq_rope.reshape(-1, heads * rope),
                          _swap_halves(wq_rope, rope).reshape(-1, heads * rope)], axis=1).astype(BF16)
    wuk = jnp.transpose(lw["w_uk"], (1, 2, 0)).astype(BF16)
    wuv = jnp.transpose(lw["w_uv"], (1, 0, 2)).astype(BF16)
    woa = lw["w_oa"].astype(BF16)
    wob = lw["w_ob"].astype(BF16)
    wout = lw["w_out"].astype(BF16)
    n_experts = lw["w_router"].shape[1]
    wr = jnp.pad(lw["w_router"], ((0, 0), (0, LANES - n_experts)))
    wr_hi = wr.astype(BF16)
    wr_lo = (wr - wr_hi.astype(F32)).astype(BF16)
    br = jnp.pad(lw["b_router"], (0, LANES - n_experts)).reshape(1, LANES)
    row1 = lambda g: g.reshape(1, -1)

    cos_t, sin_t = _rope_tables(pos, rope, heads)
    (qabs, qrope, ckv32, ckv16, kpe32, kpe16, mq16, mk32, mv32, mk16, mv16, kmean) = _proj_call(
        x, cos_t, sin_t, row1(lw["g_attn"]), wa, row1(lw["g_q"]), wq, row1(lw["g_kv"]), wuk, dims)
    kmean = kmean.reshape(t // TOKEN_TILE, kvh * dh)

    gh = m_heads // kvh
    slopes = 2.0 ** (-(8.0 / m_heads) * jnp.arange(1, m_heads + 1, dtype=F32))
    slope_p = jnp.repeat(slopes.reshape(kvh, gh), MOBA_BLOCK, axis=1).reshape(kvh, gh * MOBA_BLOCK, 1)
    lat_p = _mla_prompt_call(qabs, qrope, ckv16, kpe16, batch, seq, dims)
    mo_p = _moba_prompt_call(mq16, mk16, mv16, kmean, slope_p, batch, seq, dims)

    npad = 16
    pad_new = lambda a: jnp.pad(a[tp:].reshape(dec_batch, dec_seq, -1), ((0, 0), (0, npad - dec_seq), (0, 0)))
    qa_s = qabs[tp:].reshape(dec_batch, dec_seq * heads, kv_lora)
    qr_s = qrope[tp:].reshape(dec_batch, dec_seq * heads, rope)
    lat_s = _mla_decode_call(page_table, qa_s, qr_s, pad_new(ckv16), pad_new(kpe16), cache_ckv, cache_kpe,
                             layer, dims, dec_seq)
    q_m = mq16[tp:].reshape(dec_batch, dec_seq, kvh, gh, dh).transpose(0, 2, 3, 1, 4)
    q_m = q_m.reshape(dec_batch, kvh, gh * dec_seq, dh)
    slope_s = jnp.repeat(slopes.reshape(kvh, gh), dec_seq, axis=1).reshape(kvh, gh * dec_seq, 1)
    merged = cache_k.shape[:3] + (kvh * dh,)
    mo_s = _moba_decode_call(page_table, q_m, pad_new(mk16), pad_new(mv16), slope_s, cache_k.reshape(merged),
                             cache_v.reshape(merged), layer, dec_seq)
    mo_s = mo_s.reshape(dec_batch, kvh, gh, dec_seq, dh).transpose(0, 3, 1, 2, 4).reshape(dec_batch * dec_seq, -1)


    gated = _merge1_call(x, lat_p.reshape(tp, heads * kv_lora), lat_s.reshape(-1, heads * kv_lora).astype(BF16),
                         mo_p, mo_s.astype(BF16), row1(lw["g_attn"]), wg, wuv, woa, wob, dims)
    x2, h2, selw = _merge2_call(gated, x, wout, row1(lw["g_ffn"]), wr_hi, wr_lo, br, n_experts)

    tile_expert, n_used, row_tok, pos4, w4 = _route(selw[:, :n_experts], n_experts, EXPERT_ROWS)
    xs = _gather_call(row_tok, h2)
    yb = _expert_call(tile_expert, n_used, xs, moe_w, layer)
    return x2, yb, pos4, w4, (ckv32, kpe32, mk32, mv32)


def kernel(x_prompt, x_sample, cache_ckv, cache_kpe, cache_k, cache_v, page_table, g_attn, w_in, g_q, w_q_up, g_kv,
           w_uk, w_uv, w_oa, w_ob, w_out, g_ffn, w_router, b_router, w_gate_up, b_gate_up, w_down, b_down, g_final):
    batch, seq, d = x_prompt.shape
    dec_batch, dec_seq, _ = x_sample.shape
    depth = w_in.shape[0]
    n_pages = page_table.shape[1]
    page = cache_ckv.shape[2]
    past = n_pages * page
    kv_lora, heads, nope = w_uk.shape[1:]
    rope = w_q_up.shape[2] // heads - nope
    kvh, dh = cache_k.shape[3:]
    dims = dict(heads=heads, nope=nope, rope=rope, kv_lora=kv_lora, q_lora=w_q_up.shape[1],
                m_heads=w_ob.shape[1] // dh, m_kvh=kvh, dh=dh, mla_scale=(nope + rope) ** -0.5)
    dims["n_mq"] = dims["m_heads"] * dh
    dims["n_kv"] = kvh * dh
    tp, ts = batch * seq, dec_batch * dec_seq
    assert seq % MOBA_BLOCK == 0 and past % MOBA_BLOCK == 0 and dec_seq <= 16
    assert tp % TOKEN_TILE == 0 and ts % TOKEN_TILE == 0 and n_pages % min(DECODE_PAGES, n_pages) == 0
    assert seq % min(MLA_K_TILE, seq) == 0 and min(MLA_K_TILE, seq) % min(MLA_Q_TOKENS, seq) == 0

    x = jnp.concatenate([x_prompt.reshape(tp, d), x_sample.reshape(ts, d)], axis=0)
    pos = jnp.concatenate([jnp.tile(jnp.arange(seq, dtype=jnp.int32), batch),
                           jnp.tile(past + jnp.arange(dec_seq, dtype=jnp.int32), dec_batch)])
    outs = [[] for _ in range(8)]
    for layer in range(depth):
        lw = dict(g_attn=g_attn[layer], w_in=w_in[layer], g_q=g_q[layer], w_q_up=w_q_up[layer], g_kv=g_kv[layer],
                  w_uk=w_uk[layer], w_uv=w_uv[layer], w_oa=w_oa[layer], w_ob=w_ob[layer], w_out=w_out[layer],
                  g_ffn=g_ffn[layer], w_router=w_router[layer], b_router=b_router[layer])
        moe_w = (w_gate_up, b_gate_up, w_down, b_down)
        x2, yb, pos4, w4, (ckv, kpe, mk, mv) = _layer(
            x, pos, batch, seq, dec_batch, dec_seq, (cache_ckv, cache_kpe, cache_k, cache_v), page_table,
            lw, moe_w, layer, dims)
        final = layer == depth - 1
        x = _combine_call(pos4, w4, x2, g_final.reshape(1, d), yb, final)
        outs[0].append(ckv[:tp].reshape(batch, seq, kv_lora))
        outs[1].append(kpe[:tp].reshape(batch, seq, rope))
        outs[2].append(mk[:tp].reshape(batch, seq, kvh, dh))
        outs[3].append(mv[:tp].reshape(batch, seq, kvh, dh))
        outs[4].append(ckv[tp:].reshape(dec_batch, dec_seq, kv_lora))
        outs[5].append(kpe[tp:].reshape(dec_batch, dec_seq, rope))
        outs[6].append(mk[tp:].reshape(dec_batch, dec_seq, kvh, dh))
        outs[7].append(mv[tp:].reshape(dec_batch, dec_seq, kvh, dh))
    y_prompt = x[:tp].reshape(batch, seq, d)
    y_sample = x[tp:].reshape(dec_batch, dec_seq, d)
    return (y_prompt, y_sample) + tuple(jnp.stack(o) for o in outs)
```

```python
import functools

import jax
import jax.numpy as jnp
from jax import lax
from jax.experimental import pallas as pl
from jax.experimental.pallas import tpu as pltpu

F32 = jnp.float32
BF16 = jnp.bfloat16

NORM_EPS = 1e-6
ROPE_THETA = 10000.0
MOBA_BLOCK = 256
MOBA_TOPK = 3
MOE_TOP_K = 4
SWIGLU_LIMIT = 7.0
SWIGLU_ALPHA = 1.702
NEG_INF = -1e30

LANES = 128
VMEM_LIMIT = 56 * 1024 * 1024

TOKEN_TILE = 256
MLA_Q_TOKENS = 128
MLA_K_TILE = 512
DECODE_PAGES = 16
EXPERT_ROWS = 512
EXPERT_SUB_ROWS = 128
EXPERT_FF = 512
COMBINE_TOKENS = 128


def _rms(x, g):
    return x * lax.rsqrt(jnp.mean(x * x, axis=-1, keepdims=True) + NORM_EPS) * g


def _dot(a, b):
    return jnp.dot(a, b, preferred_element_type=F32)


def _dot_nt(a, b):
    return lax.dot_general(a, b, (((1,), (1,)), ((), ())), preferred_element_type=F32)


def _topk_onehot(vals, k, valid):
    n = vals.shape[-1]
    iota = lax.broadcasted_iota(jnp.int32, vals.shape, vals.ndim - 1).astype(F32)
    work = jnp.where(valid, vals, -jnp.inf)
    sel = jnp.zeros(vals.shape, F32)
    for _ in range(k):
        m = jnp.max(work, axis=-1, keepdims=True)
        idx = jnp.min(jnp.where(work == m, iota, float(n)), axis=-1, keepdims=True)
        hit = iota == idx
        sel = jnp.where(hit, 1.0, sel)
        work = jnp.where(hit, -jnp.inf, work)
    return jnp.where(valid, sel, 0.0)


def _params(sem, vmem=VMEM_LIMIT):
    return pltpu.CompilerParams(dimension_semantics=sem, vmem_limit_bytes=vmem)


def _const_spec(shape):
    nd = len(shape)
    return pl.BlockSpec(shape, lambda *_: (0,) * nd, pipeline_mode=pl.Buffered(1))


def _proj_body(x_ref, cos_ref, sin_ref, gat_ref, wa_ref, gq_ref, wq_ref, gkv_ref, wuk_ref,
               qabs_ref, qrope_ref, ckv32_ref, ckv16_ref, kpe32_ref, kpe16_ref,
               mq_ref, mk32_ref, mv32_ref, mk16_ref, mv16_ref, kmean_ref,
               *, q_lora, kv_lora, rope, n_mq, n_kv, heads, nope, scale):
    hb = _rms(x_ref[...], gat_ref[...]).astype(BF16)
    off = [0]

    def proj(n):
        r = _dot(hb, wa_ref[:, off[0]:off[0] + n])
        off[0] += n
        return r

    q_lat = proj(q_lora)
    c_raw = proj(kv_lora)
    mq = proj(n_mq)
    mk = proj(n_kv)
    mv = proj(n_kv)
    kpe_a = proj(LANES)
    kpe_b = proj(LANES)

    cos = cos_ref[...]
    sin = sin_ref[...]
    ckv = _rms(c_raw, gkv_ref[...])
    ckv32_ref[...] = ckv
    ckv16_ref[...] = ckv.astype(BF16)
    kpe = kpe_a * cos + kpe_b * sin
    kpe32_ref[...] = kpe[:, :rope]
    kpe16_ref[...] = kpe.astype(BF16)
    mq_ref[...] = mq.astype(BF16)
    mk32_ref[...] = mk
    mv32_ref[...] = mv
    mk16_ref[...] = mk.astype(BF16)
    mv16_ref[...] = mv.astype(BF16)
    kmean_ref[...] = jnp.sum(mk, axis=0, keepdims=True) * (1.0 / MOBA_BLOCK)

    qn = _rms(q_lat, gq_ref[...]).astype(BF16)
    hn = heads * nope
    hr = heads * LANES
    q_nope = _dot(qn, wq_ref[:, :hn])
    q_ra = _dot(qn, wq_ref[:, hn:hn + hr])
    q_rb = _dot(qn, wq_ref[:, hn + hr:hn + 2 * hr])
    for h in range(heads):
        lanes = slice(h * LANES, (h + 1) * LANES)
        qrope_ref[:, lanes] = ((q_ra[:, lanes] * cos + q_rb[:, lanes] * sin) * scale).astype(BF16)
        qa = _dot(q_nope[:, h * nope:(h + 1) * nope].astype(BF16), wuk_ref[h])
        qabs_ref[:, h * kv_lora:(h + 1) * kv_lora] = (qa * scale).astype(BF16)


def _proj_call(x, cos_t, sin_t, g_attn, wa, g_q, wq, g_kv, wuk, dims):
    t, d = x.shape
    tm = TOKEN_TILE
    heads, nope, rope = dims["heads"], dims["nope"], dims["rope"]
    q_lora, kv_lora, n_mq, n_kv = dims["q_lora"], dims["kv_lora"], dims["n_mq"], dims["n_kv"]
    nt = t // tm
    row = lambda w: pl.BlockSpec((tm, w), lambda i: (i, 0))
    out_shape = (
        jax.ShapeDtypeStruct((t, heads * kv_lora), BF16),
        jax.ShapeDtypeStruct((t, heads * LANES), BF16),
        jax.ShapeDtypeStruct((t, kv_lora), F32),
        jax.ShapeDtypeStruct((t, kv_lora), BF16),
        jax.ShapeDtypeStruct((t, rope), F32),
        jax.ShapeDtypeStruct((t, LANES), BF16),
        jax.ShapeDtypeStruct((t, n_mq), BF16),
        jax.ShapeDtypeStruct((t, n_kv), F32),
        jax.ShapeDtypeStruct((t, n_kv), F32),
        jax.ShapeDtypeStruct((t, n_kv), BF16),
        jax.ShapeDtypeStruct((t, n_kv), BF16),
        jax.ShapeDtypeStruct((nt, 1, n_kv), F32),
    )
    out_specs = (row(heads * kv_lora), row(heads * LANES), row(kv_lora), row(kv_lora), row(rope), row(LANES),
                 row(n_mq), row(n_kv), row(n_kv), row(n_kv), row(n_kv),
                 pl.BlockSpec((None, 1, n_kv), lambda i: (i, 0, 0)))
    in_specs = [row(d), row(LANES), row(LANES), _const_spec(g_attn.shape), _const_spec(wa.shape),
                _const_spec(g_q.shape), _const_spec(wq.shape), _const_spec(g_kv.shape), _const_spec(wuk.shape)]
    body = functools.partial(_proj_body, q_lora=q_lora, kv_lora=kv_lora, rope=rope, n_mq=n_mq, n_kv=n_kv,
                             heads=heads, nope=nope, scale=dims["mla_scale"])
    return pl.pallas_call(body, grid=(nt,), in_specs=in_specs, out_specs=out_specs, out_shape=out_shape,
                          compiler_params=_params(("arbitrary",)), name="proj")(
                              x, cos_t, sin_t, g_attn, wa, g_q, wq, g_kv, wuk)


def _mla_prompt_body(qa_ref, qr_ref, ckv_ref, kpe_ref, out_ref, m_scr, l_scr, acc_scr, *, tq, tk, heads):
    qi = pl.program_id(1)
    kv_lora = ckv_ref.shape[1]
    qa = jnp.concatenate([qa_ref[:, h * kv_lora:(h + 1) * kv_lora] for h in range(heads)], axis=0)
    qr = jnp.concatenate([qr_ref[:, h * LANES:(h + 1) * LANES] for h in range(heads)], axis=0)
    m_scr[...] = jnp.full(m_scr.shape, NEG_INF, F32)
    l_scr[...] = jnp.zeros(l_scr.shape, F32)
    acc_scr[...] = jnp.zeros(acc_scr.shape, F32)
    tok = qi * tq + jnp.concatenate([lax.broadcasted_iota(jnp.int32, (tq, 1), 0)] * heads, axis=0)
    n_k = lax.div(qi * tq, tk) + 1

    def step(kt, carry):
        start = pl.multiple_of(kt * tk, tk)
        kc = ckv_ref[pl.ds(start, tk), :]
        kr = kpe_ref[pl.ds(start, tk), :]
        s = _dot_nt(qa, kc) + _dot_nt(qr, kr)
        key = start + lax.broadcasted_iota(jnp.int32, (1, tk), 1)
        s = jnp.where(key <= tok, s, NEG_INF)
        m_old = m_scr[...]
        m_new = jnp.maximum(m_old, jnp.max(s, axis=-1, keepdims=True))
        alpha = jnp.exp(m_old - m_new)
        p = jnp.exp(s - m_new)
        l_scr[...] = alpha * l_scr[...] + jnp.sum(p, axis=-1, keepdims=True)
        acc_scr[...] = alpha * acc_scr[...] + _dot(p.astype(BF16), kc)
        m_scr[...] = m_new
        return carry

    lax.fori_loop(0, n_k, step, 0)
    o = acc_scr[...] / l_scr[...]
    for h in range(heads):
        out_ref[:, h * kv_lora:(h + 1) * kv_lora] = o[h * tq:(h + 1) * tq, :].astype(BF16)


def _mla_prompt_call(qabs, qrope, ckv16, kpe16, batch, seq, dims):
    heads, kv_lora = dims["heads"], dims["kv_lora"]
    tq = min(MLA_Q_TOKENS, seq)
    tk = min(MLA_K_TILE, seq)
    nq = seq // tq
    rows = tq * heads
    body = functools.partial(_mla_prompt_body, tq=tq, tk=tk, heads=heads)
    return pl.pallas_call(
        body, grid=(batch, nq),
        in_specs=[pl.BlockSpec((tq, heads * kv_lora), lambda b, q: (b * nq + q, 0)),
                  pl.BlockSpec((tq, heads * LANES), lambda b, q: (b * nq + q, 0)),
                  pl.BlockSpec((seq, kv_lora), lambda b, q: (b, 0)),
                  pl.BlockSpec((seq, LANES), lambda b, q: (b, 0))],
        out_specs=pl.BlockSpec((tq, heads * kv_lora), lambda b, q: (b * nq + q, 0)),
        out_shape=jax.ShapeDtypeStruct((batch * seq, heads * kv_lora), BF16),
        scratch_shapes=[pltpu.VMEM((rows, 1), F32), pltpu.VMEM((rows, 1), F32), pltpu.VMEM((rows, kv_lora), F32)],
        compiler_params=_params(("arbitrary", "arbitrary")), name="mla_prompt")(qabs, qrope, ckv16, kpe16)


def _moba_prompt_body(q_ref, k_ref, v_ref, kmean_ref, slope_ref, out_ref, m_scr, l_scr, acc_scr,
                      *, gh, dh, nb, scale):
    qi = pl.program_id(2)
    blk = MOBA_BLOCK
    rows = gh * blk
    q = jnp.concatenate([q_ref[:, j * dh:(j + 1) * dh] for j in range(gh)], axis=0)
    slope = slope_ref[...]
    tpos = lax.broadcasted_iota(jnp.int32, (blk, 1), 0)
    qpos = qi * blk + jnp.concatenate([tpos] * gh, axis=0)
    qposf = qpos.astype(F32)

    gate = _dot_nt(q, kmean_ref[...].astype(BF16))
    biota = lax.broadcasted_iota(jnp.int32, (rows, nb), 1)
    sel = _topk_onehot(gate, MOBA_TOPK, biota < qi)

    def scores(kt):
        start = pl.multiple_of(kt * blk, blk)
        kb = k_ref[pl.ds(start, blk), :]
        vb = v_ref[pl.ds(start, blk), :]
        kpos = start + lax.broadcasted_iota(jnp.int32, (1, blk), 1)
        s = _dot_nt(q, kb) * scale - slope * (qposf - kpos.astype(F32))
        return s, vb, kpos

    s, vb, kpos = scores(qi)
    s = jnp.where(kpos <= qpos, s, NEG_INF)
    m0 = jnp.max(s, axis=-1, keepdims=True)
    p = jnp.exp(s - m0)
    m_scr[...] = m0
    l_scr[...] = jnp.sum(p, axis=-1, keepdims=True)
    acc_scr[...] = _dot(p.astype(BF16), vb)

    sel_b = sel.astype(BF16)
    block_id = lax.broadcasted_iota(jnp.int32, (nb, blk), 0)

    def step(kt, carry):
        s, vb, _ = scores(kt)
        chosen = _dot(sel_b, jnp.where(block_id == kt, 1.0, 0.0).astype(BF16))
        s = jnp.where(chosen > 0.5, s, NEG_INF)
        m_old = m_scr[...]
        m_new = jnp.maximum(m_old, jnp.max(s, axis=-1, keepdims=True))
        alpha = jnp.exp(m_old - m_new)
        p = jnp.exp(s - m_new)
        l_scr[...] = alpha * l_scr[...] + jnp.sum(p, axis=-1, keepdims=True)
        acc_scr[...] = alpha * acc_scr[...] + _dot(p.astype(BF16), vb)
        m_scr[...] = m_new
        return carry

    lax.fori_loop(0, qi, step, 0)
    o = acc_scr[...] / l_scr[...]
    for j in range(gh):
        out_ref[:, j * dh:(j + 1) * dh] = o[j * blk:(j + 1) * blk, :].astype(BF16)


def _moba_prompt_call(mq16, mk16, mv16, kmean, slope_rows, batch, seq, dims):
    t = mq16.shape[0]
    heads, kvh, dh = dims["m_heads"], dims["m_kvh"], dims["dh"]
    gh = heads // kvh
    blk = MOBA_BLOCK
    nb = seq // blk
    rows = gh * blk
    body = functools.partial(_moba_prompt_body, gh=gh, dh=dh, nb=nb, scale=dh ** -0.5)
    return pl.pallas_call(
        body, grid=(batch, kvh, nb),
        in_specs=[pl.BlockSpec((blk, gh * dh), lambda b, g, q: (b * nb + q, g)),
                  pl.BlockSpec((seq, dh), lambda b, g, q: (b, g)),
                  pl.BlockSpec((seq, dh), lambda b, g, q: (b, g)),
                  pl.BlockSpec((nb, dh), lambda b, g, q: (b, g)),
                  pl.BlockSpec((None, rows, 1), lambda b, g, q: (g, 0, 0))],
        out_specs=pl.BlockSpec((blk, gh * dh), lambda b, g, q: (b * nb + q, g)),
        out_shape=jax.ShapeDtypeStruct((batch * seq, heads * dh), BF16),
        scratch_shapes=[pltpu.VMEM((rows, 1), F32), pltpu.VMEM((rows, 1), F32), pltpu.VMEM((rows, dh), F32)],
        compiler_params=_params(("arbitrary", "arbitrary", "arbitrary")), name="moba_prompt")(
            mq16, mk16, mv16, kmean, slope_rows)


def _page_spec(block, layer, pages_per_step, i, n_steps=None, phase=0):
    nd = len(block)

    def index(s, j, pt):
        jj = j - phase
        if n_steps is not None:
            jj = jnp.clip(jj, 0, n_steps - 1)
        return (layer, pt[s, jj * pages_per_step + i]) + (0,) * (nd - 2)

    return pl.BlockSpec(block, index)


def _mla_decode_body(pt_ref, qa_ref, qr_ref, cnew_ref, rnew_ref, *rest, pages, heads, n_new):
    ckv_pages = rest[:pages]
    kpe_pages = rest[pages:2 * pages]
    out_ref, m_scr, l_scr, acc_scr = rest[2 * pages:]
    j = pl.program_id(1)
    qa = qa_ref[...]
    qr = qr_ref[...]
    rows = qa.shape[0]

    @pl.when(j == 0)
    def _():
        cn = cnew_ref[...]
        s = _dot_nt(qa, cn) + _dot_nt(qr, rnew_ref[...])
        tok = lax.div(lax.broadcasted_iota(jnp.int32, (rows, 1), 0), heads)
        key = lax.broadcasted_iota(jnp.int32, (1, cn.shape[0]), 1)
        s = jnp.where((key <= tok) & (key < n_new), s, NEG_INF)
        m0 = jnp.max(s, axis=-1, keepdims=True)
        p = jnp.exp(s - m0)
        m_scr[...] = m0
        l_scr[...] = jnp.sum(p, axis=-1, keepdims=True)
        acc_scr[...] = _dot(p.astype(BF16), cn)

    kc = jnp.concatenate([r[...].astype(BF16) for r in ckv_pages], axis=0)
    kr_t = jnp.concatenate([r[...].astype(BF16) for r in kpe_pages], axis=1)
    s = _dot_nt(qa, kc) + _dot(qr[:, :kr_t.shape[0]], kr_t)
    m_old = m_scr[...]
    m_new = jnp.maximum(m_old, jnp.max(s, axis=-1, keepdims=True))
    alpha = jnp.exp(m_old - m_new)
    p = jnp.exp(s - m_new)
    l_scr[...] = alpha * l_scr[...] + jnp.sum(p, axis=-1, keepdims=True)
    acc_scr[...] = alpha * acc_scr[...] + _dot(p.astype(BF16), kc)
    m_scr[...] = m_new

    @pl.when(j == pl.num_programs(1) - 1)
    def _():
        out_ref[...] = acc_scr[...] / l_scr[...]


def _mla_decode_call(page_table, qa_s, qr_s, cnew, rnew, cache_ckv, cache_kpe, layer, dims, n_new):
    ns, rows, kv_lora = qa_s.shape
    n_pages = page_table.shape[1]
    page = cache_ckv.shape[2]
    rope = cache_kpe.shape[3]
    kpe_t = jnp.swapaxes(cache_kpe, 2, 3)
    pages = min(DECODE_PAGES, n_pages)
    nj = n_pages // pages
    npad = cnew.shape[1]
    seq_spec = lambda r, w: pl.BlockSpec((None, r, w), lambda s, j, pt: (s, 0, 0))
    in_specs = [seq_spec(rows, kv_lora), seq_spec(rows, LANES), seq_spec(npad, kv_lora), seq_spec(npad, LANES)]
    in_specs += [_page_spec((None, None, page, kv_lora), layer, pages, i) for i in range(pages)]
    in_specs += [_page_spec((None, None, rope, page), layer, pages, i) for i in range(pages)]
    body = functools.partial(_mla_decode_body, pages=pages, heads=dims["heads"], n_new=n_new)
    grid_spec = pltpu.PrefetchScalarGridSpec(
        num_scalar_prefetch=1, grid=(ns, nj), in_specs=in_specs, out_specs=seq_spec(rows, kv_lora),
        scratch_shapes=[pltpu.VMEM((rows, 1), F32), pltpu.VMEM((rows, 1), F32), pltpu.VMEM((rows, kv_lora), F32)])
    return pl.pallas_call(body, grid_spec=grid_spec, out_shape=jax.ShapeDtypeStruct((ns, rows, kv_lora), F32),
                          compiler_params=_params(("arbitrary", "arbitrary")), name="mla_decode")(
                              page_table, qa_s, qr_s, cnew, rnew, *([cache_ckv] * pages), *([kpe_t] * pages))


def _moba_decode_body(pt_ref, q_ref, knew_ref, vnew_ref, slope_ref, *rest,
                      pages, page, kvh, dh, n_new, past, scale):
    k_pages = rest[:pages]
    v_pages = rest[pages:2 * pages]
    out_ref, s_scr, mean_scr, sel_scr, m_scr, l_scr, acc_scr = rest[2 * pages:]
    j = pl.program_id(1)
    nj = pl.num_programs(1) // 2
    blk = MOBA_BLOCK
    keys = pages * page
    bps = keys // blk
    nblk = mean_scr.shape[1]
    rows = q_ref.shape[1]
    tok = lax.rem(lax.broadcasted_iota(jnp.int32, (rows, 1), 0), n_new)
    qposf = (past + tok).astype(F32)

    def head_rows(page_ref, g):
        return page_ref[pl.ds(g, page, stride=kvh), :]

    @pl.when(j < nj)
    def _():
        for g in range(kvh):
            kg = jnp.concatenate([head_rows(r, g) for r in k_pages], axis=0)
            sums = jnp.sum(kg.reshape(bps, blk, dh), axis=1)
            mean_scr[g, pl.ds(pl.multiple_of(j * bps, bps), bps), :] = sums * (1.0 / blk)
            s_scr[g, j] = _dot_nt(q_ref[g], kg.astype(BF16))

    @pl.when(j == nj)
    def _():
        for g in range(kvh):
            q = q_ref[g]
            gate = _dot_nt(q, mean_scr[g].astype(BF16))
            sel_scr[g] = _topk_onehot(gate, MOBA_TOPK, jnp.full(gate.shape, True))
            kn = knew_ref[:, g * dh:(g + 1) * dh]
            key = lax.broadcasted_iota(jnp.int32, (1, kn.shape[0]), 1)
            s = _dot_nt(q, kn) * scale - slope_ref[g] * (tok - key).astype(F32)
            s = jnp.where((key <= tok) & (key < n_new), s, NEG_INF)
            m0 = jnp.max(s, axis=-1, keepdims=True)
            p = jnp.exp(s - m0)
            m_scr[g] = m0
            l_scr[g] = jnp.sum(p, axis=-1, keepdims=True)
            acc_scr[g] = _dot(p.astype(BF16), vnew_ref[:, g * dh:(g + 1) * dh])

    @pl.when(j >= nj)
    def _():
        jj = j - nj
        kpos = jj * keys + lax.broadcasted_iota(jnp.int32, (1, keys), 1)
        lo = (lax.broadcasted_iota(jnp.int32, (nblk, keys), 0) - jj * bps) * blk
        ci = lax.broadcasted_iota(jnp.int32, (nblk, keys), 1)
        expand = jnp.where((ci >= lo) & (ci < lo + blk), 1.0, 0.0).astype(BF16)
        for g in range(kvh):
            chosen = _dot(sel_scr[g].astype(BF16), expand)
            s = s_scr[g, jj] * scale - slope_ref[g] * (qposf - kpos.astype(F32))
            s = jnp.where(chosen > 0.5, s, NEG_INF)
            m_old = m_scr[g]
            m_new = jnp.maximum(m_old, jnp.max(s, axis=-1, keepdims=True))
            alpha = jnp.exp(m_old - m_new)
            p = jnp.exp(s - m_new)
            l_scr[g] = alpha * l_scr[g] + jnp.sum(p, axis=-1, keepdims=True)
            vg = jnp.concatenate([head_rows(r, g).astype(BF16) for r in v_pages], axis=0)
            acc_scr[g] = alpha * acc_scr[g] + _dot(p.astype(BF16), vg)
            m_scr[g] = m_new

    @pl.when(j == 2 * nj - 1)
    def _():
        for g in range(kvh):
            out_ref[g] = acc_scr[g] / l_scr[g]


def _moba_decode_call(page_table, q_s, knew, vnew, slope_rows, cache_k, cache_v, layer, n_new):
    ns, kvh, rows, dh = q_s.shape
    n_pages = page_table.shape[1]
    page = cache_k.shape[2]
    width = kvh * dh
    cache_k = cache_k.reshape(cache_k.shape[:2] + (page * kvh, dh))
    cache_v = cache_v.reshape(cache_v.shape[:2] + (page * kvh, dh))
    pages = min(DECODE_PAGES, n_pages)
    nj = n_pages // pages
    past = n_pages * page
    nblk = past // MOBA_BLOCK
    keys = pages * page
    npad = knew.shape[1]
    in_specs = [pl.BlockSpec((None, kvh, rows, dh), lambda s, j, pt: (s, 0, 0, 0)),
                pl.BlockSpec((None, npad, width), lambda s, j, pt: (s, 0, 0)),
                pl.BlockSpec((None, npad, width), lambda s, j, pt: (s, 0, 0)),
                pl.BlockSpec((kvh, rows, 1), lambda s, j, pt: (0, 0, 0))]
    rows_pp = page * kvh
    in_specs += [_page_spec((None, None, rows_pp, dh), layer, pages, i, n_steps=nj, phase=0) for i in range(pages)]
    in_specs += [_page_spec((None, None, rows_pp, dh), layer, pages, i, n_steps=nj, phase=nj) for i in range(pages)]
    body = functools.partial(_moba_decode_body, pages=pages, page=page, kvh=kvh, dh=dh, n_new=n_new, past=past,
                             scale=dh ** -0.5)
    grid_spec = pltpu.PrefetchScalarGridSpec(
        num_scalar_prefetch=1, grid=(ns, 2 * nj), in_specs=in_specs,
        out_specs=pl.BlockSpec((None, kvh, rows, dh), lambda s, j, pt: (s, 0, 0, 0)),
        scratch_shapes=[pltpu.VMEM((kvh, nj, rows, keys), F32), pltpu.VMEM((kvh, nblk, dh), F32),
                        pltpu.VMEM((kvh, rows, nblk), F32), pltpu.VMEM((kvh, rows, 1), F32),
                        pltpu.VMEM((kvh, rows, 1), F32), pltpu.VMEM((kvh, rows, dh), F32)])
    return pl.pallas_call(body, grid_spec=grid_spec, out_shape=jax.ShapeDtypeStruct((ns, kvh, rows, dh), F32),
                          compiler_params=_params(("arbitrary", "arbitrary")), name="moba_decode")(
                              page_table, q_s, knew, vnew, slope_rows, *([cache_k] * pages), *([cache_v] * pages))


def _merge1_body(x_ref, latp_ref, lats_ref, mop_ref, mos_ref, gat_ref, wg_ref, wuv_ref, woa_ref, wob_ref, out_ref,
                 *, heads, kv_lora, prompt_tiles):
    d = x_ref.shape[1]
    is_prompt = pl.program_id(0) < prompt_tiles
    lat = jnp.where(is_prompt, latp_ref[...], lats_ref[...])
    mo = jnp.where(is_prompt, mop_ref[...], mos_ref[...])
    hb = _rms(x_ref[...], gat_ref[...]).astype(BF16)
    ga = _dot(hb, wg_ref[:, :d])
    gb = _dot(hb, wg_ref[:, d:])
    a_in = jnp.concatenate(
        [_dot(lat[:, h * kv_lora:(h + 1) * kv_lora], wuv_ref[h]).astype(BF16) for h in range(heads)], axis=1)
    a = _dot(a_in, woa_ref[...])
    m = _dot(mo, wob_ref[...])
    out_ref[...] = (jax.nn.sigmoid(ga) * a + jax.nn.sigmoid(gb) * m).astype(BF16)


def _merge1_call(x, lat_p, lat_s, mo_p, mo_s, g_attn, wg, wuv, woa, wob, dims):
    t, d = x.shape
    tm = TOKEN_TILE
    npt = lat_p.shape[0] // tm
    row = lambda w: pl.BlockSpec((tm, w), lambda i: (i, 0))
    prow = lambda w: pl.BlockSpec((tm, w), lambda i: (jnp.minimum(i, npt - 1), 0))
    srow = lambda w: pl.BlockSpec((tm, w), lambda i: (jnp.maximum(i - npt, 0), 0))
    body = functools.partial(_merge1_body, heads=dims["heads"], kv_lora=dims["kv_lora"], prompt_tiles=npt)
    return pl.pallas_call(
        body, grid=(t // tm,),
        in_specs=[row(d), prow(lat_p.shape[1]), srow(lat_s.shape[1]), prow(mo_p.shape[1]), srow(mo_s.shape[1]),
                  _const_spec(g_attn.shape), _const_spec(wg.shape),
                  _const_spec(wuv.shape), _const_spec(woa.shape), _const_spec(wob.shape)],
        out_specs=row(d), out_shape=jax.ShapeDtypeStruct((t, d), BF16),
        compiler_params=_params(("arbitrary",)), name="merge1")(
            x, lat_p, lat_s, mo_p, mo_s, g_attn, wg, wuv, woa, wob)


def _merge2_body(gated_ref, x_ref, wout_ref, gffn_ref, wr_hi_ref, wr_lo_ref, br_ref, x2_ref, h2_ref, selw_ref,
                 *, n_experts):
    x2 = x_ref[...] + _dot(gated_ref[...], wout_ref[...])
    x2_ref[...] = x2
    h2 = _rms(x2, gffn_ref[...])
    h2_ref[...] = h2
    h_hi = h2.astype(BF16)
    h_lo = (h2 - h_hi.astype(F32)).astype(BF16)
    logits = _dot(h_hi, wr_hi_ref[...]) + _dot(h_hi, wr_lo_ref[...]) + _dot(h_lo, wr_hi_ref[...]) + br_ref[...]
    valid = lax.broadcasted_iota(jnp.int32, logits.shape, 1) < n_experts
    sel = _topk_onehot(logits, MOE_TOP_K, valid)
    top = jnp.max(jnp.where(valid, logits, -jnp.inf), axis=-1, keepdims=True)
    e = jnp.where(sel > 0.0, jnp.exp(logits - top), 0.0)
    w = e / jnp.sum(e, axis=-1, keepdims=True)
    selw_ref[...] = jnp.where(sel > 0.0, w, -1.0)


def _merge2_call(gated, x, wout, g_ffn, wr_hi, wr_lo, br, n_experts):
    t, d = x.shape
    tm = TOKEN_TILE
    row = lambda w: pl.BlockSpec((tm, w), lambda i: (i, 0))
    body = functools.partial(_merge2_body, n_experts=n_experts)
    return pl.pallas_call(
        body, grid=(t // tm,),
        in_specs=[row(d), row(d), _const_spec(wout.shape), _const_spec(g_ffn.shape), _const_spec(wr_hi.shape),
                  _const_spec(wr_lo.shape), _const_spec(br.shape)],
        out_specs=(row(d), row(d), row(LANES)),
        out_shape=(jax.ShapeDtypeStruct((t, d), F32), jax.ShapeDtypeStruct((t, d), F32),
                   jax.ShapeDtypeStruct((t, LANES), F32)),
        compiler_params=_params(("arbitrary",)), name="merge2")(gated, x, wout, g_ffn, wr_hi, wr_lo, br)


def _row_copy(src_hbm, src_row, dst, dst_row, sem):
    return pltpu.make_async_copy(src_hbm.at[pl.ds(src_row, 1), :], dst.at[pl.ds(dst_row, 1), :], sem)


def _tile_copy(src_hbm, dst, sem):
    return pltpu.make_async_copy(src_hbm.at[pl.ds(0, dst.shape[0]), :], dst, sem)


def _expert_body(te_ref, nu_ref, tb_ref, tv_ref, cur_a, cur_b, nxt_a, nxt_b, h_hbm,
                 wg_ref, wu_ref, bg_ref, bu_ref, wd_ref, bd_ref, out_ref, xg_scr, xb_scr, sem, *, sub_rows):
    i = pl.program_id(0)
    f = pl.program_id(1)
    n_used = nu_ref[0]
    tm = xb_scr.shape[0]
    slot = lax.rem(i, 2)

    def gather(tile_idx, blk_a, blk_b, s):
        off = lax.rem(tb_ref[tile_idx], tm)
        valid = tv_ref[tile_idx]

        def one(r, carry):
            idx = off + r
            tok = jnp.where(idx < tm, blk_a[0, jnp.minimum(idx, tm - 1)], blk_b[0, jnp.maximum(idx - tm, 0)])
            tok = jnp.where(r < valid, tok, 0)
            _row_copy(h_hbm, tok, xg_scr.at[s], r, sem.at[s]).start()
            return carry

        lax.fori_loop(0, tm, one, 0, unroll=8)

    @pl.when((f == 0) & (i == 0))
    def _():
        gather(0, cur_a, cur_b, 0)

    @pl.when((f == 0) & (i + 1 < n_used))
    def _():
        gather(i + 1, nxt_a, nxt_b, 1 - slot)

    @pl.when((f == 0) & (i >= n_used))
    def _():
        out_ref[...] = jnp.zeros(out_ref.shape, F32)

    @pl.when(i < n_used)
    def _():
        @pl.when(f == 0)
        def _():
            _tile_copy(h_hbm, xg_scr.at[slot], sem.at[slot]).wait()
            xb_scr[...] = xg_scr[slot].astype(BF16)
            out_ref[...] = jnp.broadcast_to(bd_ref[...], out_ref.shape)

        wg = wg_ref[...].astype(BF16)
        wu = wu_ref[...].astype(BF16)
        wd = wd_ref[...].astype(BF16)
        bg = bg_ref[...]
        bu = bu_ref[...]
        for r0 in range(0, tm, sub_rows):
            rows = slice(r0, r0 + sub_rows)
            xb = xb_scr[rows, :]
            gate = jnp.minimum(_dot(xb, wg) + bg, SWIGLU_LIMIT)
            up = jnp.clip(_dot(xb, wu) + bu, -SWIGLU_LIMIT, SWIGLU_LIMIT)
            act = (up + 1.0) * gate * jax.nn.sigmoid(SWIGLU_ALPHA * gate)
            out_ref[rows, :] += _dot(act.astype(BF16), wd)


def _expert_call(route, h2, moe_w, layer):
    tile_expert, n_used, tile_base, tile_valid, sorted_blocks = route
    w_gate_up, b_gate_up, w_down, b_down = moe_w
    d = h2.shape[1]
    depth, n_exp, _, two_f = w_gate_up.shape
    ff = two_f // 2
    tm = EXPERT_ROWS
    tf = min(EXPERT_FF, ff)
    nf = ff // tf
    nt = tile_expert.shape[0]
    nblk = sorted_blocks.shape[0]
    bgu = b_gate_up.reshape(depth, n_exp, 1, two_f)
    bd = b_down.reshape(depth, n_exp, 1, d)

    def tile(i, nu):
        return jnp.minimum(i, nu[0] - 1)

    def expert(i, te, nu):
        return te[tile(i, nu)]

    def col(i, f, nu):
        return jnp.where(i < nu[0], f, nf - 1)

    def tok_block(ahead, second):
        def index(i, f, te, nu, tb, tv):
            blk = lax.div(tb[tile(i + ahead, nu)], tm) + second
            return (jnp.minimum(blk, nblk - 1), 0, 0)
        return pl.BlockSpec((None, 1, tm), index, memory_space=pltpu.SMEM)

    wspec = lambda shape, index: pl.BlockSpec((None, None) + shape, index)
    in_specs = [
        tok_block(0, 0), tok_block(0, 1), tok_block(1, 0), tok_block(1, 1),
        pl.BlockSpec(memory_space=pl.ANY),
        wspec((d, tf), lambda i, f, te, nu, tb, tv: (layer, expert(i, te, nu), 0, col(i, f, nu))),
        wspec((d, tf), lambda i, f, te, nu, tb, tv: (layer, expert(i, te, nu), 0, nf + col(i, f, nu))),
        wspec((1, tf), lambda i, f, te, nu, tb, tv: (layer, expert(i, te, nu), 0, col(i, f, nu))),
        wspec((1, tf), lambda i, f, te, nu, tb, tv: (layer, expert(i, te, nu), 0, nf + col(i, f, nu))),
        wspec((tf, d), lambda i, f, te, nu, tb, tv: (layer, expert(i, te, nu), col(i, f, nu), 0)),
        wspec((1, d), lambda i, f, te, nu, tb, tv: (layer, expert(i, te, nu), 0, 0)),
    ]
    grid_spec = pltpu.PrefetchScalarGridSpec(
        num_scalar_prefetch=4, grid=(nt, nf), in_specs=in_specs,
        out_specs=pl.BlockSpec((tm, d), lambda i, f, te, nu, tb, tv: (i, 0)),
        scratch_shapes=[pltpu.VMEM((2, tm, d), F32), pltpu.VMEM((tm, d), BF16), pltpu.SemaphoreType.DMA((2,))])
    body = functools.partial(_expert_body, sub_rows=min(EXPERT_SUB_ROWS, tm))
    return pl.pallas_call(
        body, grid_spec=grid_spec, out_shape=jax.ShapeDtypeStruct((nt * tm, d), F32),
        compiler_params=_params(("arbitrary", "arbitrary")), name="moe_experts")(
            tile_expert, n_used, tile_base, tile_valid, sorted_blocks, sorted_blocks, sorted_blocks, sorted_blocks,
            h2, w_gate_up, w_gate_up, bgu, bgu, w_down, bd)


def _combine_body(pos_cur_ref, pos_next_ref, w_ref, x2_ref, gf_ref, yb_hbm, y_ref, buf, sem, *, tm, top_k, final):
    i = pl.program_id(0)
    n = pl.num_programs(0)
    slot = lax.rem(i, 2)

    def issue(pos_ref, s):
        for k in range(top_k):
            def one(r, carry, k=k):
                _row_copy(yb_hbm, pos_ref[0, k * tm + r], buf.at[s, k], r, sem.at[s]).start()
                return carry
            lax.fori_loop(0, tm, one, 0, unroll=8)

    @pl.when(i == 0)
    def _():
        issue(pos_cur_ref, 0)

    @pl.when(i + 1 < n)
    def _():
        issue(pos_next_ref, 1 - slot)

    for k in range(top_k):
        _tile_copy(yb_hbm, buf.at[slot, k], sem.at[slot]).wait()

    w = w_ref[...]
    moe = w[:, 0:1] * buf[slot, 0]
    for k in range(1, top_k):
        moe = moe + w[:, k:k + 1] * buf[slot, k]
    y = x2_ref[...] + moe
    y_ref[...] = _rms(y, gf_ref[...]) if final else y


def _combine_call(pos4, w4, x2, g_final, yb, final):
    t, d = x2.shape
    top_k = pos4.shape[1]
    tm = COMBINE_TOKENS
    nt = t // tm
    pos_tiles = pos4.reshape(nt, tm, top_k).transpose(0, 2, 1).reshape(nt, 1, top_k * tm)
    body = functools.partial(_combine_body, tm=tm, top_k=top_k, final=final)
    smem = lambda index: pl.BlockSpec((None, 1, top_k * tm), index, memory_space=pltpu.SMEM)
    return pl.pallas_call(
        body, grid=(nt,),
        in_specs=[smem(lambda i: (i, 0, 0)), smem(lambda i: (jnp.minimum(i + 1, nt - 1), 0, 0)),
                  pl.BlockSpec((tm, top_k), lambda i: (i, 0)), pl.BlockSpec((tm, d), lambda i: (i, 0)),
                  _const_spec(g_final.shape), pl.BlockSpec(memory_space=pl.ANY)],
        out_specs=pl.BlockSpec((tm, d), lambda i: (i, 0)),
        out_shape=jax.ShapeDtypeStruct((t, d), F32),
        scratch_shapes=[pltpu.VMEM((2, top_k, tm, d), F32), pltpu.SemaphoreType.DMA((2,))],
        compiler_params=_params(("arbitrary",)), name="moe_combine")(pos_tiles, pos_tiles, w4, x2, g_final, yb)


def _route(selw, n_experts, tile_rows):
    t = selw.shape[0]
    i32 = jnp.int32
    chosen = selw >= 0.0
    sel = chosen.astype(i32)
    counts = jnp.sum(sel, axis=0)
    tiles_per = (counts + tile_rows - 1) // tile_rows
    tile_end = jnp.cumsum(tiles_per)
    tile_start = tile_end - tiles_per
    n_used = tile_end[-1]
    starts = jnp.cumsum(counts) - counts
    n_pairs = t * MOE_TOP_K
    n_tiles = -(-n_pairs // tile_rows) + n_experts
    ti = jnp.arange(n_tiles, dtype=i32)
    tile_expert = jnp.minimum(jnp.sum((ti[:, None] >= tile_end[None, :]).astype(i32), axis=1), n_experts - 1)
    is_e = tile_expert[:, None] == jnp.arange(n_experts, dtype=i32)[None, :]
    pick = lambda v: jnp.sum(jnp.where(is_e, v[None, :], 0), axis=1)
    tile_in_e = ti - pick(tile_start)
    tile_base = pick(starts) + tile_in_e * tile_rows
    tile_valid = jnp.clip(pick(counts) - tile_in_e * tile_rows, 0, tile_rows)
    pos = (tile_start * tile_rows)[None, :] + jnp.cumsum(sel, axis=0) - 1
    kth = jnp.cumsum(sel, axis=1)
    e_iota = jnp.arange(n_experts, dtype=i32)[None, :]
    pos4, w4, e4 = [], [], []
    for k in range(MOE_TOP_K):
        m = chosen & (kth == k + 1)
        pos4.append(jnp.sum(jnp.where(m, pos, 0), axis=1))
        w4.append(jnp.sum(jnp.where(m, selw, 0.0), axis=1))
        e4.append(jnp.sum(jnp.where(m, e_iota, 0), axis=1))
    pos4, w4, e4 = (jnp.stack(a, axis=1) for a in (pos4, w4, e4))
    order = jnp.argsort(e4.reshape(-1))
    sorted_tok = (order // MOE_TOP_K).astype(i32)
    n_blocks = -(-n_pairs // tile_rows) + 1
    sorted_blocks = jnp.pad(sorted_tok, (0, n_blocks * tile_rows - n_pairs)).reshape(n_blocks, 1, tile_rows)
    route = (tile_expert.astype(i32), n_used.reshape(1).astype(i32), tile_base.astype(i32),
             tile_valid.astype(i32), sorted_blocks)
    return route, pos4.astype(i32), w4


def _rope_tables(pos, rope):
    half = rope // 2
    inv = ROPE_THETA ** (-jnp.arange(half, dtype=F32) / half)
    ang = pos.astype(F32)[:, None] * inv[None, :]
    cos, sin = jnp.cos(ang), jnp.sin(ang)
    zero = jnp.zeros((pos.shape[0], LANES - rope), F32)
    return jnp.concatenate([cos, cos, zero], axis=1), jnp.concatenate([-sin, sin, zero], axis=1)


def _pad_lanes(w):
    return jnp.pad(w, [(0, 0)] * (w.ndim - 1) + [(0, LANES - w.shape[-1])])


def _swap_halves(w, rope):
    half = rope // 2
    return jnp.concatenate([w[..., half:], w[..., :half]], axis=-1)


def _layer(x, pos, batch, seq, dec_batch, dec_seq, caches, page_table, lw, moe_w, layer, dims):
    cache_ckv, cache_kpe, cache_k, cache_v = caches
    t, d = x.shape
    heads, nope, rope, kv_lora = dims["heads"], dims["nope"], dims["rope"], dims["kv_lora"]
    m_heads, kvh, dh = dims["m_heads"], dims["m_kvh"], dims["dh"]
    tp = batch * seq

    w_in = lw["w_in"]
    sizes = (dims["q_lora"], kv_lora, rope, m_heads * dh, kvh * dh, kvh * dh, d, d)
    offs = [0]
    for n in sizes:
        offs.append(offs[-1] + n)
    col = lambda k: w_in[:, offs[k]:offs[k + 1]]
    wa = jnp.concatenate([col(0), col(1), col(3), col(4), col(5), _pad_lanes(col(2)),
                          _pad_lanes(_swap_halves(col(2), rope))], axis=1).astype(BF16)
    wg = jnp.concatenate([col(6), col(7)], axis=1).astype(BF16)
    wq3 = lw["w_q_up"].reshape(dims["q_lora"], heads, nope + rope)
    wq_rope = wq3[:, :, nope:]
    wq = jnp.concatenate([wq3[:, :, :nope].reshape(-1, heads * nope),
                          _pad_lanes(wq_rope).reshape(-1, heads * LANES),
                          _pad_lanes(_swap_halves(wq_rope, rope)).reshape(-1, heads * LANES)], axis=1).astype(BF16)
    wuk = jnp.transpose(lw["w_uk"], (1, 2, 0)).astype(BF16)
    wuv = jnp.transpose(lw["w_uv"], (1, 0, 2)).astype(BF16)
    woa = lw["w_oa"].astype(BF16)
    wob = lw["w_ob"].astype(BF16)
    wout = lw["w_out"].astype(BF16)
    n_experts = lw["w_router"].shape[1]
    wr = jnp.pad(lw["w_router"], ((0, 0), (0, LANES - n_experts)))
    wr_hi = wr.astype(BF16)
    wr_lo = (wr - wr_hi.astype(F32)).astype(BF16)
    br = jnp.pad(lw["b_router"], (0, LANES - n_experts)).reshape(1, LANES)
    row1 = lambda g: g.reshape(1, -1)

    cos_t, sin_t = _rope_tables(pos, rope)
    (qabs, qrope, ckv32, ckv16, kpe32, kpe16, mq16, mk32, mv32, mk16, mv16, kmean) = _proj_call(
        x, cos_t, sin_t, row1(lw["g_attn"]), wa, row1(lw["g_q"]), wq, row1(lw["g_kv"]), wuk, dims)
    kmean = kmean.reshape(t // TOKEN_TILE, kvh * dh)

    gh = m_heads // kvh
    slopes = 2.0 ** (-(8.0 / m_heads) * jnp.arange(1, m_heads + 1, dtype=F32))
    slope_p = jnp.repeat(slopes.reshape(kvh, gh), MOBA_BLOCK, axis=1).reshape(kvh, gh * MOBA_BLOCK, 1)
    lat_p = _mla_prompt_call(qabs, qrope, ckv16, kpe16, batch, seq, dims)
    mo_p = _moba_prompt_call(mq16, mk16, mv16, kmean, slope_p, batch, seq, dims)

    npad = 16
    pad_new = lambda a: jnp.pad(a[tp:].reshape(dec_batch, dec_seq, -1), ((0, 0), (0, npad - dec_seq), (0, 0)))
    qa_s = qabs[tp:].reshape(dec_batch, dec_seq * heads, kv_lora)
    qr_s = qrope[tp:].reshape(dec_batch, dec_seq * heads, LANES)
    lat_s = _mla_decode_call(page_table, qa_s, qr_s, pad_new(ckv16), pad_new(kpe16), cache_ckv, cache_kpe,
                             layer, dims, dec_seq)
    q_m = mq16[tp:].reshape(dec_batch, dec_seq, kvh, gh, dh).transpose(0, 2, 3, 1, 4)
    q_m = q_m.reshape(dec_batch, kvh, gh * dec_seq, dh)
    slope_s = jnp.repeat(slopes.reshape(kvh, gh), dec_seq, axis=1).reshape(kvh, gh * dec_seq, 1)
    mo_s = _moba_decode_call(page_table, q_m, pad_new(mk16), pad_new(mv16), slope_s, cache_k, cache_v,
                             layer, dec_seq)
    mo_s = mo_s.reshape(dec_batch, kvh, gh, dec_seq, dh).transpose(0, 3, 1, 2, 4).reshape(dec_batch * dec_seq, -1)

    gated = _merge1_call(x, lat_p, lat_s.reshape(-1, heads * kv_lora).astype(BF16),
                         mo_p, mo_s.astype(BF16), row1(lw["g_attn"]), wg, wuv, woa, wob, dims)
    x2, h2, selw = _merge2_call(gated, x, wout, row1(lw["g_ffn"]), wr_hi, wr_lo, br, n_experts)

    route, pos4, w4 = _route(selw[:, :n_experts], n_experts, EXPERT_ROWS)
    yb = _expert_call(route, h2, moe_w, layer)
    return x2, yb, pos4, w4, (ckv32, kpe32, mk32, mv32)


def kernel(x_prompt, x_sample, cache_ckv, cache_kpe, cache_k, cache_v, page_table, g_attn, w_in, g_q, w_q_up, g_kv,
           w_uk, w_uv, w_oa, w_ob, w_out, g_ffn, w_router, b_router, w_gate_up, b_gate_up, w_down, b_down, g_final):
    batch, seq, d = x_prompt.shape
    dec_batch, dec_seq, _ = x_sample.shape
    depth = w_in.shape[0]
    n_pages = page_table.shape[1]
    page = cache_ckv.shape[2]
    past = n_pages * page
    kv_lora, heads, nope = w_uk.shape[1:]
    rope = w_q_up.shape[2] // heads - nope
    kvh, dh = cache_k.shape[3:]
    dims = dict(heads=heads, nope=nope, rope=rope, kv_lora=kv_lora, q_lora=w_q_up.shape[1],
                m_heads=w_ob.shape[1] // dh, m_kvh=kvh, dh=dh, mla_scale=(nope + rope) ** -0.5)
    dims["n_mq"] = dims["m_heads"] * dh
    dims["n_kv"] = kvh * dh
    tp, ts = batch * seq, dec_batch * dec_seq
    assert seq % MOBA_BLOCK == 0 and past % MOBA_BLOCK == 0 and dec_seq <= 16
    assert tp % TOKEN_TILE == 0 and ts % TOKEN_TILE == 0 and n_pages % min(DECODE_PAGES, n_pages) == 0
    assert seq % min(MLA_K_TILE, seq) == 0 and min(MLA_K_TILE, seq) % min(MLA_Q_TOKENS, seq) == 0

    x = jnp.concatenate([x_prompt.reshape(tp, d), x_sample.reshape(ts, d)], axis=0)
    pos = jnp.concatenate([jnp.tile(jnp.arange(seq, dtype=jnp.int32), batch),
                           jnp.tile(past + jnp.arange(dec_seq, dtype=jnp.int32), dec_batch)])
    outs = [[] for _ in range(8)]
    for layer in range(depth):
        lw = dict(g_attn=g_attn[layer], w_in=w_in[layer], g_q=g_q[layer], w_q_up=w_q_up[layer], g_kv=g_kv[layer],
                  w_uk=w_uk[layer], w_uv=w_uv[layer], w_oa=w_oa[layer], w_ob=w_ob[layer], w_out=w_out[layer],
                  g_ffn=g_ffn[layer], w_router=w_router[layer], b_router=b_router[layer])
        moe_w = (w_gate_up, b_gate_up, w_down, b_down)
        x2, yb, pos4, w4, (ckv, kpe, mk, mv) = _layer(
            x, pos, batch, seq, dec_batch, dec_seq, (cache_ckv, cache_kpe, cache_k, cache_v), page_table,
            lw, moe_w, layer, dims)
        final = layer == depth - 1
        x = _combine_call(pos4, w4, x2, g_final.reshape(1, d), yb, final)
        outs[0].append(ckv[:tp].reshape(batch, seq, kv_lora))
        outs[1].append(kpe[:tp].reshape(batch, seq, rope))
        outs[2].append(mk[:tp].reshape(batch, seq, kvh, dh))
        outs[3].append(mv[:tp].reshape(batch, seq, kvh, dh))
        outs[4].append(ckv[tp:].reshape(dec_batch, dec_seq, kv_lora))
        outs[5].append(kpe[tp:].reshape(dec_batch, dec_seq, rope))
        outs[6].append(mk[tp:].reshape(dec_batch, dec_seq, kvh, dh))
        outs[7].append(mv[tp:].reshape(dec_batch, dec_seq, kvh, dh))
    y_prompt = x[:tp].reshape(batch, seq, d)
    y_sample = x[tp:].reshape(dec_batch, dec_seq, d)
    return (y_prompt, y_sample) + tuple(jnp.stack(o) for o in outs)
```

```python
import functools

import jax
import jax.numpy as jnp
from jax import lax
from jax.experimental import pallas as pl
from jax.experimental.pallas import tpu as pltpu

F32 = jnp.float32
BF16 = jnp.bfloat16

NORM_EPS = 1e-6
ROPE_THETA = 10000.0
MOBA_BLOCK = 256
MOBA_TOPK = 3
MOE_TOP_K = 4
SWIGLU_LIMIT = 7.0
SWIGLU_ALPHA = 1.702
NEG_INF = -1e30

LANES = 128
VMEM_LIMIT = 56 * 1024 * 1024

TOKEN_TILE = 256
MLA_Q_TOKENS = 128
MLA_K_TILE = 512
DECODE_PAGES = 16
EXPERT_ROWS = 512
EXPERT_SUB_ROWS = 128
EXPERT_FF = 512
COMBINE_TOKENS = 128


def _rms(x, g):
    return x * lax.rsqrt(jnp.mean(x * x, axis=-1, keepdims=True) + NORM_EPS) * g


def _dot(a, b):
    return jnp.dot(a, b, preferred_element_type=F32)


def _dot_nt(a, b):
    return lax.dot_general(a, b, (((1,), (1,)), ((), ())), preferred_element_type=F32)


def _topk_onehot(vals, k, valid, axis=-1):
    axis = axis % vals.ndim
    n = vals.shape[axis]
    iota = lax.broadcasted_iota(jnp.int32, vals.shape, axis).astype(F32)
    work = jnp.where(valid, vals, -jnp.inf)
    sel = jnp.zeros(vals.shape, F32)
    for _ in range(k):
        m = jnp.max(work, axis=axis, keepdims=True)
        idx = jnp.min(jnp.where(work == m, iota, float(n)), axis=axis, keepdims=True)
        hit = iota == idx
        sel = jnp.where(hit, 1.0, sel)
        work = jnp.where(hit, -jnp.inf, work)
    return jnp.where(valid, sel, 0.0)


def _params(sem, vmem=VMEM_LIMIT):
    return pltpu.CompilerParams(dimension_semantics=sem, vmem_limit_bytes=vmem)


def _const_spec(shape):
    nd = len(shape)
    return pl.BlockSpec(shape, lambda *_: (0,) * nd, pipeline_mode=pl.Buffered(1))


def _proj_body(x_ref, cos_ref, sin_ref, gat_ref, wa_ref, gq_ref, wq_ref, gkv_ref, wuk_ref,
               qabs_ref, qrope_ref, ckv32_ref, ckv16_ref, kpe32_ref, kpe16_ref,
               mq_ref, mk32_ref, mv32_ref, mk16_ref, mv16_ref, kmean_ref, ckvt_ref, mvt_ref,
               *, q_lora, kv_lora, rope, n_mq, n_kv, heads, nope, scale):
    hb = _rms(x_ref[...], gat_ref[...]).astype(BF16)
    off = [0]

    def proj(n):
        r = _dot(hb, wa_ref[:, off[0]:off[0] + n])
        off[0] += n
        return r

    q_lat = proj(q_lora)
    c_raw = proj(kv_lora)
    mq = proj(n_mq)
    mk = proj(n_kv)
    mv = proj(n_kv)
    kpe_a = proj(LANES)
    kpe_b = proj(LANES)

    cos = cos_ref[...]
    sin = sin_ref[...]
    ckv = _rms(c_raw, gkv_ref[...])
    ckv32_ref[...] = ckv
    ckv16_ref[...] = ckv.astype(BF16)
    kpe = kpe_a * cos + kpe_b * sin
    kpe32_ref[...] = kpe[:, :rope]
    kpe16_ref[...] = kpe.astype(BF16)
    mq_ref[...] = mq.astype(BF16)
    mk32_ref[...] = mk
    mv32_ref[...] = mv
    mk16_ref[...] = mk.astype(BF16)
    mv16_ref[...] = mv.astype(BF16)
    kmean_ref[...] = jnp.sum(mk, axis=0, keepdims=True) * (1.0 / MOBA_BLOCK)
    ckvt_ref[...] = ckv.T.astype(BF16)
    mvt_ref[...] = mv.T.astype(BF16)

    qn = _rms(q_lat, gq_ref[...]).astype(BF16)
    hn = heads * nope
    hr = heads * LANES
    q_nope = _dot(qn, wq_ref[:, :hn])
    q_ra = _dot(qn, wq_ref[:, hn:hn + hr])
    q_rb = _dot(qn, wq_ref[:, hn + hr:hn + 2 * hr])
    for h in range(heads):
        lanes = slice(h * LANES, (h + 1) * LANES)
        qrope_ref[:, lanes] = ((q_ra[:, lanes] * cos + q_rb[:, lanes] * sin) * scale).astype(BF16)
        qa = _dot(q_nope[:, h * nope:(h + 1) * nope].astype(BF16), wuk_ref[h])
        qabs_ref[:, h * kv_lora:(h + 1) * kv_lora] = (qa * scale).astype(BF16)


def _proj_call(x, cos_t, sin_t, g_attn, wa, g_q, wq, g_kv, wuk, dims):
    t, d = x.shape
    tm = TOKEN_TILE
    heads, nope, rope = dims["heads"], dims["nope"], dims["rope"]
    q_lora, kv_lora, n_mq, n_kv = dims["q_lora"], dims["kv_lora"], dims["n_mq"], dims["n_kv"]
    nt = t // tm
    row = lambda w: pl.BlockSpec((tm, w), lambda i: (i, 0))
    out_shape = (
        jax.ShapeDtypeStruct((t, heads * kv_lora), BF16),
        jax.ShapeDtypeStruct((t, heads * LANES), BF16),
        jax.ShapeDtypeStruct((t, kv_lora), F32),
        jax.ShapeDtypeStruct((t, kv_lora), BF16),
        jax.ShapeDtypeStruct((t, rope), F32),
        jax.ShapeDtypeStruct((t, LANES), BF16),
        jax.ShapeDtypeStruct((t, n_mq), BF16),
        jax.ShapeDtypeStruct((t, n_kv), F32),
        jax.ShapeDtypeStruct((t, n_kv), F32),
        jax.ShapeDtypeStruct((t, n_kv), BF16),
        jax.ShapeDtypeStruct((t, n_kv), BF16),
        jax.ShapeDtypeStruct((nt, 1, n_kv), F32),
        jax.ShapeDtypeStruct((nt, kv_lora, tm), BF16),
        jax.ShapeDtypeStruct((nt, n_kv, tm), BF16),
    )
    tile3 = lambda w: pl.BlockSpec((None, w, tm), lambda i: (i, 0, 0))
    out_specs = (row(heads * kv_lora), row(heads * LANES), row(kv_lora), row(kv_lora), row(rope), row(LANES),
                 row(n_mq), row(n_kv), row(n_kv), row(n_kv), row(n_kv),
                 pl.BlockSpec((None, 1, n_kv), lambda i: (i, 0, 0)), tile3(kv_lora), tile3(n_kv))
    in_specs = [row(d), row(LANES), row(LANES), _const_spec(g_attn.shape), _const_spec(wa.shape),
                _const_spec(g_q.shape), _const_spec(wq.shape), _const_spec(g_kv.shape), _const_spec(wuk.shape)]
    body = functools.partial(_proj_body, q_lora=q_lora, kv_lora=kv_lora, rope=rope, n_mq=n_mq, n_kv=n_kv,
                             heads=heads, nope=nope, scale=dims["mla_scale"])
    return pl.pallas_call(body, grid=(nt,), in_specs=in_specs, out_specs=out_specs, out_shape=out_shape,
                          compiler_params=_params(("arbitrary",)), name="proj")(
                              x, cos_t, sin_t, g_attn, wa, g_q, wq, g_kv, wuk)


def _mla_prompt_body(qa_ref, qr_ref, ckv_ref, kpe_ref, ckvt_ref, out_ref, m_scr, l_scr, acc_scr,
                     *, tq, tk, heads):
    qi = pl.program_id(1)
    kv_lora = ckv_ref.shape[1]
    sub = ckvt_ref.shape[2]
    qa = jnp.concatenate([qa_ref[:, h * kv_lora:(h + 1) * kv_lora] for h in range(heads)], axis=0)
    qr = jnp.concatenate([qr_ref[:, h * LANES:(h + 1) * LANES] for h in range(heads)], axis=0)
    m_scr[...] = jnp.full(m_scr.shape, NEG_INF, F32)
    l_scr[...] = jnp.zeros(l_scr.shape, F32)
    acc_scr[...] = jnp.zeros(acc_scr.shape, F32)
    tok = qi * tq + jnp.concatenate([lax.broadcasted_iota(jnp.int32, (1, tq), 1)] * heads, axis=1)
    n_k = lax.div(qi * tq, tk) + 1

    def step(kt, carry):
        start = pl.multiple_of(kt * tk, tk)
        kc = ckv_ref[pl.ds(start, tk), :]
        kr = kpe_ref[pl.ds(start, tk), :]
        s = _dot_nt(kc, qa) + _dot_nt(kr, qr)
        key = start + lax.broadcasted_iota(jnp.int32, (tk, 1), 0)
        s = jnp.where(key <= tok, s, NEG_INF)
        m_old = m_scr[...]
        m_new = jnp.maximum(m_old, jnp.max(s, axis=0, keepdims=True))
        alpha = jnp.exp(m_old - m_new)
        p = jnp.exp(s - m_new)
        l_scr[...] = alpha * l_scr[...] + jnp.sum(p, axis=0, keepdims=True)
        pb = p.astype(BF16)
        pv = _dot(ckvt_ref[kt * (tk // sub)], pb[:sub, :])
        for c in range(1, tk // sub):
            pv = pv + _dot(ckvt_ref[kt * (tk // sub) + c], pb[c * sub:(c + 1) * sub, :])
        acc_scr[...] = alpha * acc_scr[...] + pv
        m_scr[...] = m_new
        return carry

    lax.fori_loop(0, n_k, step, 0)
    o_t = acc_scr[...] / l_scr[...]
    for h in range(heads):
        out_ref[:, h * kv_lora:(h + 1) * kv_lora] = o_t[:, h * tq:(h + 1) * tq].T.astype(BF16)


def _mla_prompt_call(qabs, qrope, ckv16, kpe16, ckvt, batch, seq, dims):
    heads, kv_lora = dims["heads"], dims["kv_lora"]
    tq = min(MLA_Q_TOKENS, seq)
    tk = min(MLA_K_TILE, seq)
    nq = seq // tq
    rows = tq * heads
    sub = ckvt.shape[2]
    body = functools.partial(_mla_prompt_body, tq=tq, tk=tk, heads=heads)
    return pl.pallas_call(
        body, grid=(batch, nq),
        in_specs=[pl.BlockSpec((tq, heads * kv_lora), lambda b, q: (b * nq + q, 0)),
                  pl.BlockSpec((tq, heads * LANES), lambda b, q: (b * nq + q, 0)),
                  pl.BlockSpec((seq, kv_lora), lambda b, q: (b, 0)),
                  pl.BlockSpec((seq, LANES), lambda b, q: (b, 0)),
                  pl.BlockSpec((seq // sub, kv_lora, sub), lambda b, q: (b, 0, 0))],
        out_specs=pl.BlockSpec((tq, heads * kv_lora), lambda b, q: (b * nq + q, 0)),
        out_shape=jax.ShapeDtypeStruct((batch * seq, heads * kv_lora), BF16),
        scratch_shapes=[pltpu.VMEM((1, rows), F32), pltpu.VMEM((1, rows), F32), pltpu.VMEM((kv_lora, rows), F32)],
        compiler_params=_params(("arbitrary", "arbitrary")), name="mla_prompt")(qabs, qrope, ckv16, kpe16, ckvt)


def _moba_prompt_body(q_ref, k_ref, vt_ref, kmean_ref, slope_ref, out_ref, sel_scr, m_scr, l_scr, acc_scr,
                      *, gh, dh, nb, scale):
    qi = pl.program_id(2)
    blk = MOBA_BLOCK
    rows = gh * blk
    q = jnp.concatenate([q_ref[:, j * dh:(j + 1) * dh] for j in range(gh)], axis=0)
    slope = slope_ref[...]
    tpos = lax.broadcasted_iota(jnp.int32, (1, blk), 1)
    qpos = qi * blk + jnp.concatenate([tpos] * gh, axis=1)
    qposf = qpos.astype(F32)

    gate = _dot_nt(kmean_ref[...].astype(BF16), q)
    biota = lax.broadcasted_iota(jnp.int32, (nb, rows), 0)
    sel_scr[...] = _topk_onehot(gate, MOBA_TOPK, biota < qi, axis=0)

    def scores(kt):
        start = pl.multiple_of(kt * blk, blk)
        kb = k_ref[pl.ds(start, blk), :]
        kpos = start + lax.broadcasted_iota(jnp.int32, (blk, 1), 0)
        s = _dot_nt(kb, q) * scale - slope * (qposf - kpos.astype(F32))
        return s, vt_ref[kt], kpos

    s, vt, kpos = scores(qi)
    s = jnp.where(kpos <= qpos, s, NEG_INF)
    m0 = jnp.max(s, axis=0, keepdims=True)
    p = jnp.exp(s - m0)
    m_scr[...] = m0
    l_scr[...] = jnp.sum(p, axis=0, keepdims=True)
    acc_scr[...] = _dot(vt, p.astype(BF16))

    def step(kt, carry):
        s, vt, _ = scores(kt)
        chosen = sel_scr[pl.ds(kt, 1), :]
        s = jnp.where(chosen > 0.5, s, NEG_INF)
        m_old = m_scr[...]
        m_new = jnp.maximum(m_old, jnp.max(s, axis=0, keepdims=True))
        alpha = jnp.exp(m_old - m_new)
        p = jnp.exp(s - m_new)
        l_scr[...] = alpha * l_scr[...] + jnp.sum(p, axis=0, keepdims=True)
        acc_scr[...] = alpha * acc_scr[...] + _dot(vt, p.astype(BF16))
        m_scr[...] = m_new
        return carry

    lax.fori_loop(0, qi, step, 0)
    o_t = acc_scr[...] / l_scr[...]
    for j in range(gh):
        out_ref[:, j * dh:(j + 1) * dh] = o_t[:, j * blk:(j + 1) * blk].T.astype(BF16)


def _moba_prompt_call(mq16, mk16, mvt, kmean, slope_rows, batch, seq, dims):
    heads, kvh, dh = dims["m_heads"], dims["m_kvh"], dims["dh"]
    gh = heads // kvh
    blk = MOBA_BLOCK
    nb = seq // blk
    rows = gh * blk
    body = functools.partial(_moba_prompt_body, gh=gh, dh=dh, nb=nb, scale=dh ** -0.5)
    return pl.pallas_call(
        body, grid=(batch, kvh, nb),
        in_specs=[pl.BlockSpec((blk, gh * dh), lambda b, g, q: (b * nb + q, g)),
                  pl.BlockSpec((seq, dh), lambda b, g, q: (b, g)),
                  pl.BlockSpec((nb, dh, blk), lambda b, g, q: (b, g, 0)),
                  pl.BlockSpec((nb, dh), lambda b, g, q: (b, g)),
                  pl.BlockSpec((None, 1, rows), lambda b, g, q: (g, 0, 0))],
        out_specs=pl.BlockSpec((blk, gh * dh), lambda b, g, q: (b * nb + q, g)),
        out_shape=jax.ShapeDtypeStruct((batch * seq, heads * dh), BF16),
        scratch_shapes=[pltpu.VMEM((nb, rows), F32), pltpu.VMEM((1, rows), F32), pltpu.VMEM((1, rows), F32),
                        pltpu.VMEM((dh, rows), F32)],
        compiler_params=_params(("arbitrary", "arbitrary", "arbitrary")), name="moba_prompt")(
            mq16, mk16, mvt, kmean, slope_rows)


def _page_spec(block, layer, pages_per_step, i, n_steps=None, phase=0):
    nd = len(block)

    def index(s, j, pt):
        jj = j - phase
        if n_steps is not None:
            jj = jnp.clip(jj, 0, n_steps - 1)
        return (layer, pt[s, jj * pages_per_step + i]) + (0,) * (nd - 2)

    return pl.BlockSpec(block, index)


def _mla_decode_body(pt_ref, qa_ref, qr_ref, cnew_ref, rnew_ref, *rest, pages, heads, n_new):
    ckv_pages = rest[:pages]
    kpe_pages = rest[pages:2 * pages]
    out_ref, m_scr, l_scr, acc_scr = rest[2 * pages:]
    j = pl.program_id(1)
    qa = qa_ref[...]
    qr = qr_ref[...]
    rows = qa.shape[0]

    @pl.when(j == 0)
    def _():
        cn = cnew_ref[...]
        s = _dot_nt(qa, cn) + _dot_nt(qr, rnew_ref[...])
        tok = lax.div(lax.broadcasted_iota(jnp.int32, (rows, 1), 0), heads)
        key = lax.broadcasted_iota(jnp.int32, (1, cn.shape[0]), 1)
        s = jnp.where((key <= tok) & (key < n_new), s, NEG_INF)
        m0 = jnp.max(s, axis=-1, keepdims=True)
        p = jnp.exp(s - m0)
        m_scr[...] = m0
        l_scr[...] = jnp.sum(p, axis=-1, keepdims=True)
        acc_scr[...] = _dot(p.astype(BF16), cn)

    kc = jnp.concatenate([r[...].astype(BF16) for r in ckv_pages], axis=0)
    kr_t = jnp.concatenate([r[...].astype(BF16) for r in kpe_pages], axis=1)
    s = _dot_nt(qa, kc) + _dot(qr[:, :kr_t.shape[0]], kr_t)
    m_old = m_scr[...]
    m_new = jnp.maximum(m_old, jnp.max(s, axis=-1, keepdims=True))
    alpha = jnp.exp(m_old - m_new)
    p = jnp.exp(s - m_new)
    l_scr[...] = alpha * l_scr[...] + jnp.sum(p, axis=-1, keepdims=True)
    acc_scr[...] = alpha * acc_scr[...] + _dot(p.astype(BF16), kc)
    m_scr[...] = m_new

    @pl.when(j == pl.num_programs(1) - 1)
    def _():
        out_ref[...] = acc_scr[...] / l_scr[...]


def _mla_decode_call(page_table, qa_s, qr_s, cnew, rnew, cache_ckv, cache_kpe, layer, dims, n_new):
    ns, rows, kv_lora = qa_s.shape
    n_pages = page_table.shape[1]
    page = cache_ckv.shape[2]
    rope = cache_kpe.shape[3]
    kpe_t = jnp.swapaxes(cache_kpe, 2, 3)
    pages = min(DECODE_PAGES, n_pages)
    nj = n_pages // pages
    npad = cnew.shape[1]
    seq_spec = lambda r, w: pl.BlockSpec((None, r, w), lambda s, j, pt: (s, 0, 0))
    in_specs = [seq_spec(rows, kv_lora), seq_spec(rows, LANES), seq_spec(npad, kv_lora), seq_spec(npad, LANES)]
    in_specs += [_page_spec((None, None, page, kv_lora), layer, pages, i) for i in range(pages)]
    in_specs += [_page_spec((None, None, rope, page), layer, pages, i) for i in range(pages)]
    body = functools.partial(_mla_decode_body, pages=pages, heads=dims["heads"], n_new=n_new)
    grid_spec = pltpu.PrefetchScalarGridSpec(
        num_scalar_prefetch=1, grid=(ns, nj), in_specs=in_specs, out_specs=seq_spec(rows, kv_lora),
        scratch_shapes=[pltpu.VMEM((rows, 1), F32), pltpu.VMEM((rows, 1), F32), pltpu.VMEM((rows, kv_lora), F32)])
    return pl.pallas_call(body, grid_spec=grid_spec, out_shape=jax.ShapeDtypeStruct((ns, rows, kv_lora), F32),
                          compiler_params=_params(("arbitrary", "arbitrary")), name="mla_decode")(
                              page_table, qa_s, qr_s, cnew, rnew, *([cache_ckv] * pages), *([kpe_t] * pages))


def _moba_decode_body(pt_ref, q_ref, knew_ref, vnew_ref, slope_ref, *rest,
                      pages, page, kvh, dh, n_new, past, scale, n_steps):
    k_pages = rest[:pages]
    v_pages = rest[pages:2 * pages]
    out_ref, s_scr, p_scr, mean_scr, l_scr, acc_scr = rest[2 * pages:]
    j = pl.program_id(1)
    nj = n_steps
    blk = MOBA_BLOCK
    keys = pages * page
    bps = keys // blk
    nblk = mean_scr.shape[1]
    rows = q_ref.shape[1]
    tok = lax.rem(lax.broadcasted_iota(jnp.int32, (rows, 1), 0), n_new)
    qposf = (past + tok).astype(F32)

    def head_rows(page_ref, g):
        return page_ref[pl.ds(g, page, stride=kvh), :]

    @pl.when(j < nj)
    def _():
        for g in range(kvh):
            kg = jnp.concatenate([head_rows(r, g) for r in k_pages], axis=0)
            sums = jnp.sum(kg.reshape(bps, blk, dh), axis=1)
            mean_scr[g, pl.ds(pl.multiple_of(j * bps, bps), bps), :] = sums * (1.0 / blk)
            s_scr[g, j] = _dot_nt(q_ref[g], kg.astype(BF16))

    @pl.when(j == nj)
    def _():
        ci = lax.broadcasted_iota(jnp.int32, (nblk, keys), 1)
        for g in range(kvh):
            q = q_ref[g]
            slope = slope_ref[g]
            gate = _dot_nt(q, mean_scr[g].astype(BF16))
            sel = _topk_onehot(gate, MOBA_TOPK, jnp.full(gate.shape, True)).astype(BF16)
            kn = knew_ref[:, g * dh:(g + 1) * dh]
            key = lax.broadcasted_iota(jnp.int32, (1, kn.shape[0]), 1)
            s_new = _dot_nt(q, kn) * scale - slope * (tok - key).astype(F32)
            s_new = jnp.where((key <= tok) & (key < n_new), s_new, NEG_INF)
            m = jnp.max(s_new, axis=-1, keepdims=True)
            for jj in range(n_steps):
                lo = (lax.broadcasted_iota(jnp.int32, (nblk, keys), 0) - jj * bps) * blk
                expand = jnp.where((ci >= lo) & (ci < lo + blk), 1.0, 0.0).astype(BF16)
                chosen = _dot(sel, expand)
                kpos = jj * keys + lax.broadcasted_iota(jnp.int32, (1, keys), 1)
                s = s_scr[g, jj] * scale - slope * (qposf - kpos.astype(F32))
                s = jnp.where(chosen > 0.5, s, NEG_INF)
                s_scr[g, jj] = s
                m = jnp.maximum(m, jnp.max(s, axis=-1, keepdims=True))
            p_new = jnp.exp(s_new - m)
            l = jnp.sum(p_new, axis=-1, keepdims=True)
            for jj in range(n_steps):
                p = jnp.exp(s_scr[g, jj] - m)
                l = l + jnp.sum(p, axis=-1, keepdims=True)
                p_scr[g, jj] = p.astype(BF16)
            l_scr[g] = l
            acc_scr[g] = _dot(p_new.astype(BF16), vnew_ref[:, g * dh:(g + 1) * dh])

    @pl.when(j >= nj)
    def _():
        jj = j - nj
        for g in range(kvh):
            vg = jnp.concatenate([head_rows(r, g).astype(BF16) for r in v_pages], axis=0)
            acc_scr[g] += _dot(p_scr[g, jj], vg)

    @pl.when(j == 2 * nj - 1)
    def _():
        for g in range(kvh):
            out_ref[g] = acc_scr[g] / l_scr[g]


def _moba_decode_call(page_table, q_s, knew, vnew, slope_rows, cache_k, cache_v, layer, n_new):
    ns, kvh, rows, dh = q_s.shape
    n_pages = page_table.shape[1]
    page = cache_k.shape[2]
    width = kvh * dh
    cache_k = cache_k.reshape(cache_k.shape[:2] + (page * kvh, dh))
    cache_v = cache_v.reshape(cache_v.shape[:2] + (page * kvh, dh))
    pages = min(DECODE_PAGES, n_pages)
    nj = n_pages // pages
    past = n_pages * page
    nblk = past // MOBA_BLOCK
    keys = pages * page
    npad = knew.shape[1]
    in_specs = [pl.BlockSpec((None, kvh, rows, dh), lambda s, j, pt: (s, 0, 0, 0)),
                pl.BlockSpec((None, npad, width), lambda s, j, pt: (s, 0, 0)),
                pl.BlockSpec((None, npad, width), lambda s, j, pt: (s, 0, 0)),
                pl.BlockSpec((kvh, rows, 1), lambda s, j, pt: (0, 0, 0))]
    rows_pp = page * kvh
    in_specs += [_page_spec((None, None, rows_pp, dh), layer, pages, i, n_steps=nj, phase=0) for i in range(pages)]
    in_specs += [_page_spec((None, None, rows_pp, dh), layer, pages, i, n_steps=nj, phase=nj) for i in range(pages)]
    body = functools.partial(_moba_decode_body, pages=pages, page=page, kvh=kvh, dh=dh, n_new=n_new, past=past,
                             scale=dh ** -0.5, n_steps=nj)
    grid_spec = pltpu.PrefetchScalarGridSpec(
        num_scalar_prefetch=1, grid=(ns, 2 * nj), in_specs=in_specs,
        out_specs=pl.BlockSpec((None, kvh, rows, dh), lambda s, j, pt: (s, 0, 0, 0)),
        scratch_shapes=[pltpu.VMEM((kvh, nj, rows, keys), F32), pltpu.VMEM((kvh, nj, rows, keys), BF16),
                        pltpu.VMEM((kvh, nblk, dh), F32), pltpu.VMEM((kvh, rows, 1), F32),
                        pltpu.VMEM((kvh, rows, dh), F32)])
    return pl.pallas_call(body, grid_spec=grid_spec, out_shape=jax.ShapeDtypeStruct((ns, kvh, rows, dh), F32),
                          compiler_params=_params(("arbitrary", "arbitrary")), name="moba_decode")(
                              page_table, q_s, knew, vnew, slope_rows, *([cache_k] * pages), *([cache_v] * pages))


def _merge1_body(x_ref, latp_ref, lats_ref, mop_ref, mos_ref, gat_ref, wg_ref, wuv_ref, woa_ref, wob_ref, out_ref,
                 *, heads, kv_lora, prompt_tiles):
    d = x_ref.shape[1]
    is_prompt = pl.program_id(0) < prompt_tiles
    lat = jnp.where(is_prompt, latp_ref[...], lats_ref[...])
    mo = jnp.where(is_prompt, mop_ref[...], mos_ref[...])
    hb = _rms(x_ref[...], gat_ref[...]).astype(BF16)
    ga = _dot(hb, wg_ref[:, :d])
    gb = _dot(hb, wg_ref[:, d:])
    a_in = jnp.concatenate(
        [_dot(lat[:, h * kv_lora:(h + 1) * kv_lora], wuv_ref[h]).astype(BF16) for h in range(heads)], axis=1)
    a = _dot(a_in, woa_ref[...])
    m = _dot(mo, wob_ref[...])
    out_ref[...] = (jax.nn.sigmoid(ga) * a + jax.nn.sigmoid(gb) * m).astype(BF16)


def _merge1_call(x, lat_p, lat_s, mo_p, mo_s, g_attn, wg, wuv, woa, wob, dims):
    t, d = x.shape
    tm = TOKEN_TILE
    npt = lat_p.shape[0] // tm
    row = lambda w: pl.BlockSpec((tm, w), lambda i: (i, 0))
    prow = lambda w: pl.BlockSpec((tm, w), lambda i: (jnp.minimum(i, npt - 1), 0))
    srow = lambda w: pl.BlockSpec((tm, w), lambda i: (jnp.maximum(i - npt, 0), 0))
    body = functools.partial(_merge1_body, heads=dims["heads"], kv_lora=dims["kv_lora"], prompt_tiles=npt)
    return pl.pallas_call(
        body, grid=(t // tm,),
        in_specs=[row(d), prow(lat_p.shape[1]), srow(lat_s.shape[1]), prow(mo_p.shape[1]), srow(mo_s.shape[1]),
                  _const_spec(g_attn.shape), _const_spec(wg.shape),
                  _const_spec(wuv.shape), _const_spec(woa.shape), _const_spec(wob.shape)],
        out_specs=row(d), out_shape=jax.ShapeDtypeStruct((t, d), BF16),
        compiler_params=_params(("arbitrary",)), name="merge1")(
            x, lat_p, lat_s, mo_p, mo_s, g_attn, wg, wuv, woa, wob)


def _merge2_body(gated_ref, x_ref, wout_ref, gffn_ref, wr_hi_ref, wr_lo_ref, br_ref, x2_ref, h2_ref, selw_ref,
                 *, n_experts):
    x2 = x_ref[...] + _dot(gated_ref[...], wout_ref[...])
    x2_ref[...] = x2
    h2 = _rms(x2, gffn_ref[...])
    h2_ref[...] = h2
    h_hi = h2.astype(BF16)
    h_lo = (h2 - h_hi.astype(F32)).astype(BF16)
    logits = _dot(h_hi, wr_hi_ref[...]) + _dot(h_hi, wr_lo_ref[...]) + _dot(h_lo, wr_hi_ref[...]) + br_ref[...]
    valid = lax.broadcasted_iota(jnp.int32, logits.shape, 1) < n_experts
    sel = _topk_onehot(logits, MOE_TOP_K, valid)
    top = jnp.max(jnp.where(valid, logits, -jnp.inf), axis=-1, keepdims=True)
    e = jnp.where(sel > 0.0, jnp.exp(logits - top), 0.0)
    w = e / jnp.sum(e, axis=-1, keepdims=True)
    selw_ref[...] = jnp.where(sel > 0.0, w, -1.0)


def _merge2_call(gated, x, wout, g_ffn, wr_hi, wr_lo, br, n_experts):
    t, d = x.shape
    tm = TOKEN_TILE
    row = lambda w: pl.BlockSpec((tm, w), lambda i: (i, 0))
    body = functools.partial(_merge2_body, n_experts=n_experts)
    return pl.pallas_call(
        body, grid=(t // tm,),
        in_specs=[row(d), row(d), _const_spec(wout.shape), _const_spec(g_ffn.shape), _const_spec(wr_hi.shape),
                  _const_spec(wr_lo.shape), _const_spec(br.shape)],
        out_specs=(row(d), row(d), row(LANES)),
        out_shape=(jax.ShapeDtypeStruct((t, d), F32), jax.ShapeDtypeStruct((t, d), F32),
                   jax.ShapeDtypeStruct((t, LANES), F32)),
        compiler_params=_params(("arbitrary",)), name="merge2")(gated, x, wout, g_ffn, wr_hi, wr_lo, br)


def _row_copy(src_hbm, src_row, dst, dst_row, sem):
    return pltpu.make_async_copy(src_hbm.at[pl.ds(src_row, 1), :], dst.at[pl.ds(dst_row, 1), :], sem)


def _tile_copy(src_hbm, dst, sem):
    return pltpu.make_async_copy(src_hbm.at[pl.ds(0, dst.shape[0]), :], dst, sem)


def _expert_body(te_ref, nu_ref, tb_ref, tv_ref, cur_a, cur_b, nxt_a, nxt_b, h_hbm,
                 wg_ref, wu_ref, bg_ref, bu_ref, wd_ref, bd_ref, out_ref, xg_scr, xb_scr, sem, *, sub_rows):
    i = pl.program_id(0)
    f = pl.program_id(1)
    n_used = nu_ref[0]
    tm = xb_scr.shape[0]
    slot = lax.rem(i, 2)

    def gather(tile_idx, blk_a, blk_b, s):
        off = lax.rem(tb_ref[tile_idx], tm)
        valid = tv_ref[tile_idx]

        def one(r, carry):
            idx = off + r
            tok = jnp.where(idx < tm, blk_a[0, jnp.minimum(idx, tm - 1)], blk_b[0, jnp.maximum(idx - tm, 0)])
            tok = jnp.where(r < valid, tok, 0)
            _row_copy(h_hbm, tok, xg_scr.at[s], r, sem.at[s]).start()
            return carry

        lax.fori_loop(0, tm, one, 0, unroll=8)

    @pl.when((f == 0) & (i == 0))
    def _():
        gather(0, cur_a, cur_b, 0)

    @pl.when((f == 0) & (i + 1 < n_used))
    def _():
        gather(i + 1, nxt_a, nxt_b, 1 - slot)

    @pl.when((f == 0) & (i >= n_used))
    def _():
        out_ref[...] = jnp.zeros(out_ref.shape, F32)

    @pl.when(i < n_used)
    def _():
        @pl.when(f == 0)
        def _():
            _tile_copy(h_hbm, xg_scr.at[slot], sem.at[slot]).wait()
            xb_scr[...] = xg_scr[slot].astype(BF16)
            out_ref[...] = jnp.broadcast_to(bd_ref[...], out_ref.shape)

        wg = wg_ref[...].astype(BF16)
        wu = wu_ref[...].astype(BF16)
        wd = wd_ref[...].astype(BF16)
        bg = bg_ref[...]
        bu = bu_ref[...]
        for r0 in range(0, tm, sub_rows):
            rows = slice(r0, r0 + sub_rows)
            xb = xb_scr[rows, :]
            gate = jnp.minimum(_dot(xb, wg) + bg, SWIGLU_LIMIT)
            up = jnp.clip(_dot(xb, wu) + bu, -SWIGLU_LIMIT, SWIGLU_LIMIT)
            act = (up + 1.0) * gate * jax.nn.sigmoid(SWIGLU_ALPHA * gate)
            out_ref[rows, :] += _dot(act.astype(BF16), wd)


def _expert_call(route, h2, moe_w, layer):
    tile_expert, n_used, tile_base, tile_valid, sorted_blocks = route
    w_gate_up, b_gate_up, w_down, b_down = moe_w
    d = h2.shape[1]
    depth, n_exp, _, two_f = w_gate_up.shape
    ff = two_f // 2
    tm = EXPERT_ROWS
    tf = min(EXPERT_FF, ff)
    nf = ff // tf
    nt = tile_expert.shape[0]
    nblk = sorted_blocks.shape[0]
    bgu = b_gate_up.reshape(depth, n_exp, 1, two_f)
    bd = b_down.reshape(depth, n_exp, 1, d)

    def tile(i, nu):
        return jnp.minimum(i, nu[0] - 1)

    def expert(i, te, nu):
        return te[tile(i, nu)]

    def col(i, f, nu):
        return jnp.where(i < nu[0], f, nf - 1)

    def tok_block(ahead, second):
        def index(i, f, te, nu, tb, tv):
            blk = lax.div(tb[tile(i + ahead, nu)], tm) + second
            return (jnp.minimum(blk, nblk - 1), 0, 0)
        return pl.BlockSpec((None, 1, tm), index, memory_space=pltpu.SMEM)

    wspec = lambda shape, index: pl.BlockSpec((None, None) + shape, index)
    in_specs = [
        tok_block(0, 0), tok_block(0, 1), tok_block(1, 0), tok_block(1, 1),
        pl.BlockSpec(memory_space=pl.ANY),
        wspec((d, tf), lambda i, f, te, nu, tb, tv: (layer, expert(i, te, nu), 0, col(i, f, nu))),
        wspec((d, tf), lambda i, f, te, nu, tb, tv: (layer, expert(i, te, nu), 0, nf + col(i, f, nu))),
        wspec((1, tf), lambda i, f, te, nu, tb, tv: (layer, expert(i, te, nu), 0, col(i, f, nu))),
        wspec((1, tf), lambda i, f, te, nu, tb, tv: (layer, expert(i, te, nu), 0, nf + col(i, f, nu))),
        wspec((tf, d), lambda i, f, te, nu, tb, tv: (layer, expert(i, te, nu), col(i, f, nu), 0)),
        wspec((1, d), lambda i, f, te, nu, tb, tv: (layer, expert(i, te, nu), 0, 0)),
    ]
    grid_spec = pltpu.PrefetchScalarGridSpec(
        num_scalar_prefetch=4, grid=(nt, nf), in_specs=in_specs,
        out_specs=pl.BlockSpec((tm, d), lambda i, f, te, nu, tb, tv: (i, 0)),
        scratch_shapes=[pltpu.VMEM((2, tm, d), F32), pltpu.VMEM((tm, d), BF16), pltpu.SemaphoreType.DMA((2,))])
    body = functools.partial(_expert_body, sub_rows=min(EXPERT_SUB_ROWS, tm))
    return pl.pallas_call(
        body, grid_spec=grid_spec, out_shape=jax.ShapeDtypeStruct((nt * tm, d), F32),
        compiler_params=_params(("arbitrary", "arbitrary")), name="moe_experts")(
            tile_expert, n_used, tile_base, tile_valid, sorted_blocks, sorted_blocks, sorted_blocks, sorted_blocks,
            h2, w_gate_up, w_gate_up, bgu, bgu, w_down, bd)


def _combine_body(pos_cur_ref, pos_next_ref, w_ref, x2_ref, gf_ref, yb_hbm, y_ref, buf, sem, *, tm, top_k, final):
    i = pl.program_id(0)
    n = pl.num_programs(0)
    slot = lax.rem(i, 2)

    def issue(pos_ref, s):
        for k in range(top_k):
            def one(r, carry, k=k):
                _row_copy(yb_hbm, pos_ref[0, k * tm + r], buf.at[s, k], r, sem.at[s]).start()
                return carry
            lax.fori_loop(0, tm, one, 0, unroll=8)

    @pl.when(i == 0)
    def _():
        issue(pos_cur_ref, 0)

    @pl.when(i + 1 < n)
    def _():
        issue(pos_next_ref, 1 - slot)

    for k in range(top_k):
        _tile_copy(yb_hbm, buf.at[slot, k], sem.at[slot]).wait()

    w = w_ref[...]
    moe = w[:, 0:1] * buf[slot, 0]
    for k in range(1, top_k):
        moe = moe + w[:, k:k + 1] * buf[slot, k]
    y = x2_ref[...] + moe
    y_ref[...] = _rms(y, gf_ref[...]) if final else y


def _combine_call(pos4, w4, x2, g_final, yb, final):
    t, d = x2.shape
    top_k = pos4.shape[1]
    tm = COMBINE_TOKENS
    nt = t // tm
    pos_tiles = pos4.reshape(nt, tm, top_k).transpose(0, 2, 1).reshape(nt, 1, top_k * tm)
    body = functools.partial(_combine_body, tm=tm, top_k=top_k, final=final)
    smem = lambda index: pl.BlockSpec((None, 1, top_k * tm), index, memory_space=pltpu.SMEM)
    return pl.pallas_call(
        body, grid=(nt,),
        in_specs=[smem(lambda i: (i, 0, 0)), smem(lambda i: (jnp.minimum(i + 1, nt - 1), 0, 0)),
                  pl.BlockSpec((tm, top_k), lambda i: (i, 0)), pl.BlockSpec((tm, d), lambda i: (i, 0)),
                  _const_spec(g_final.shape), pl.BlockSpec(memory_space=pl.ANY)],
        out_specs=pl.BlockSpec((tm, d), lambda i: (i, 0)),
        out_shape=jax.ShapeDtypeStruct((t, d), F32),
        scratch_shapes=[pltpu.VMEM((2, top_k, tm, d), F32), pltpu.SemaphoreType.DMA((2,))],
        compiler_params=_params(("arbitrary",)), name="moe_combine")(pos_tiles, pos_tiles, w4, x2, g_final, yb)


def _route(selw, n_experts, tile_rows):
    t = selw.shape[0]
    i32 = jnp.int32
    chosen = selw >= 0.0
    sel = chosen.astype(i32)
    counts = jnp.sum(sel, axis=0)
    tiles_per = (counts + tile_rows - 1) // tile_rows
    tile_end = jnp.cumsum(tiles_per)
    tile_start = tile_end - tiles_per
    n_used = tile_end[-1]
    starts = jnp.cumsum(counts) - counts
    n_pairs = t * MOE_TOP_K
    n_tiles = -(-n_pairs // tile_rows) + n_experts
    ti = jnp.arange(n_tiles, dtype=i32)
    tile_expert = jnp.minimum(jnp.sum((ti[:, None] >= tile_end[None, :]).astype(i32), axis=1), n_experts - 1)
    is_e = tile_expert[:, None] == jnp.arange(n_experts, dtype=i32)[None, :]
    pick = lambda v: jnp.sum(jnp.where(is_e, v[None, :], 0), axis=1)
    tile_in_e = ti - pick(tile_start)
    tile_base = pick(starts) + tile_in_e * tile_rows
    tile_valid = jnp.clip(pick(counts) - tile_in_e * tile_rows, 0, tile_rows)
    pos = (tile_start * tile_rows)[None, :] + jnp.cumsum(sel, axis=0) - 1
    kth = jnp.cumsum(sel, axis=1)
    e_iota = jnp.arange(n_experts, dtype=i32)[None, :]
    pos4, w4, e4 = [], [], []
    for k in range(MOE_TOP_K):
        m = chosen & (kth == k + 1)
        pos4.append(jnp.sum(jnp.where(m, pos, 0), axis=1))
        w4.append(jnp.sum(jnp.where(m, selw, 0.0), axis=1))
        e4.append(jnp.sum(jnp.where(m, e_iota, 0), axis=1))
    pos4, w4, e4 = (jnp.stack(a, axis=1) for a in (pos4, w4, e4))
    order = jnp.argsort(e4.reshape(-1))
    sorted_tok = (order // MOE_TOP_K).astype(i32)
    n_blocks = -(-n_pairs // tile_rows) + 1
    sorted_blocks = jnp.pad(sorted_tok, (0, n_blocks * tile_rows - n_pairs)).reshape(n_blocks, 1, tile_rows)
    route = (tile_expert.astype(i32), n_used.reshape(1).astype(i32), tile_base.astype(i32),
             tile_valid.astype(i32), sorted_blocks)
    return route, pos4.astype(i32), w4


def _rope_tables(pos, rope):
    half = rope // 2
    inv = ROPE_THETA ** (-jnp.arange(half, dtype=F32) / half)
    ang = pos.astype(F32)[:, None] * inv[None, :]
    cos, sin = jnp.cos(ang), jnp.sin(ang)
    zero = jnp.zeros((pos.shape[0], LANES - rope), F32)
    return jnp.concatenate([cos, cos, zero], axis=1), jnp.concatenate([-sin, sin, zero], axis=1)


def _pad_lanes(w):
    return jnp.pad(w, [(0, 0)] * (w.ndim - 1) + [(0, LANES - w.shape[-1])])


def _swap_halves(w, rope):
    half = rope // 2
    return jnp.concatenate([w[..., half:], w[..., :half]], axis=-1)


def _layer(x, pos, batch, seq, dec_batch, dec_seq, caches, page_table, lw, moe_w, layer, dims):
    cache_ckv, cache_kpe, cache_k, cache_v = caches
    t, d = x.shape
    heads, nope, rope, kv_lora = dims["heads"], dims["nope"], dims["rope"], dims["kv_lora"]
    m_heads, kvh, dh = dims["m_heads"], dims["m_kvh"], dims["dh"]
    tp = batch * seq

    w_in = lw["w_in"]
    sizes = (dims["q_lora"], kv_lora, rope, m_heads * dh, kvh * dh, kvh * dh, d, d)
    offs = [0]
    for n in sizes:
        offs.append(offs[-1] + n)
    col = lambda k: w_in[:, offs[k]:offs[k + 1]]
    wa = jnp.concatenate([col(0), col(1), col(3), col(4), col(5), _pad_lanes(col(2)),
                          _pad_lanes(_swap_halves(col(2), rope))], axis=1).astype(BF16)
    wg = jnp.concatenate([col(6), col(7)], axis=1).astype(BF16)
    wq3 = lw["w_q_up"].reshape(dims["q_lora"], heads, nope + rope)
    wq_rope = wq3[:, :, nope:]
    wq = jnp.concatenate([wq3[:, :, :nope].reshape(-1, heads * nope),
                          _pad_lanes(wq_rope).reshape(-1, heads * LANES),
                          _pad_lanes(_swap_halves(wq_rope, rope)).reshape(-1, heads * LANES)], axis=1).astype(BF16)
    wuk = jnp.transpose(lw["w_uk"], (1, 2, 0)).astype(BF16)
    wuv = jnp.transpose(lw["w_uv"], (1, 0, 2)).astype(BF16)
    woa = lw["w_oa"].astype(BF16)
    wob = lw["w_ob"].astype(BF16)
    wout = lw["w_out"].astype(BF16)
    n_experts = lw["w_router"].shape[1]
    wr = jnp.pad(lw["w_router"], ((0, 0), (0, LANES - n_experts)))
    wr_hi = wr.astype(BF16)
    wr_lo = (wr - wr_hi.astype(F32)).astype(BF16)
    br = jnp.pad(lw["b_router"], (0, LANES - n_experts)).reshape(1, LANES)
    row1 = lambda g: g.reshape(1, -1)

    cos_t, sin_t = _rope_tables(pos, rope)
    (qabs, qrope, ckv32, ckv16, kpe32, kpe16, mq16, mk32, mv32, mk16, mv16, kmean, ckvt, mvt) = _proj_call(
        x, cos_t, sin_t, row1(lw["g_attn"]), wa, row1(lw["g_q"]), wq, row1(lw["g_kv"]), wuk, dims)
    kmean = kmean.reshape(t // TOKEN_TILE, kvh * dh)

    gh = m_heads // kvh
    slopes = 2.0 ** (-(8.0 / m_heads) * jnp.arange(1, m_heads + 1, dtype=F32))
    slope_p = jnp.repeat(slopes.reshape(kvh, gh), MOBA_BLOCK, axis=1).reshape(kvh, 1, gh * MOBA_BLOCK)
    lat_p = _mla_prompt_call(qabs, qrope, ckv16, kpe16, ckvt, batch, seq, dims)
    mo_p = _moba_prompt_call(mq16, mk16, mvt, kmean, slope_p, batch, seq, dims)

    npad = 16
    pad_new = lambda a: jnp.pad(a[tp:].reshape(dec_batch, dec_seq, -1), ((0, 0), (0, npad - dec_seq), (0, 0)))
    qa_s = qabs[tp:].reshape(dec_batch, dec_seq * heads, kv_lora)
    qr_s = qrope[tp:].reshape(dec_batch, dec_seq * heads, LANES)
    lat_s = _mla_decode_call(page_table, qa_s, qr_s, pad_new(ckv16), pad_new(kpe16), cache_ckv, cache_kpe,
                             layer, dims, dec_seq)
    q_m = mq16[tp:].reshape(dec_batch, dec_seq, kvh, gh, dh).transpose(0, 2, 3, 1, 4)
    q_m = q_m.reshape(dec_batch, kvh, gh * dec_seq, dh)
    slope_s = jnp.repeat(slopes.reshape(kvh, gh), dec_seq, axis=1).reshape(kvh, gh * dec_seq, 1)
    mo_s = _moba_decode_call(page_table, q_m, pad_new(mk16), pad_new(mv16), slope_s, cache_k, cache_v,
                             layer, dec_seq)
    mo_s = mo_s.reshape(dec_batch, kvh, gh, dec_seq, dh).transpose(0, 3, 1, 2, 4).reshape(dec_batch * dec_seq, -1)

    gated = _merge1_call(x, lat_p, lat_s.reshape(-1, heads * kv_lora).astype(BF16),
                         mo_p, mo_s.astype(BF16), row1(lw["g_attn"]), wg, wuv, woa, wob, dims)
    x2, h2, selw = _merge2_call(gated, x, wout, row1(lw["g_ffn"]), wr_hi, wr_lo, br, n_experts)

    route, pos4, w4 = _route(selw[:, :n_experts], n_experts, EXPERT_ROWS)
    yb = _expert_call(route, h2, moe_w, layer)
    return x2, yb, pos4, w4, (ckv32, kpe32, mk32, mv32)


def kernel(x_prompt, x_sample, cache_ckv, cache_kpe, cache_k, cache_v, page_table, g_attn, w_in, g_q, w_q_up, g_kv,
           w_uk, w_uv, w_oa, w_ob, w_out, g_ffn, w_router, b_router, w_gate_up, b_gate_up, w_down, b_down, g_final):
    batch, seq, d = x_prompt.shape
    dec_batch, dec_seq, _ = x_sample.shape
    depth = w_in.shape[0]
    n_pages = page_table.shape[1]
    page = cache_ckv.shape[2]
    past = n_pages * page
    kv_lora, heads, nope = w_uk.shape[1:]
    rope = w_q_up.shape[2] // heads - nope
    kvh, dh = cache_k.shape[3:]
    dims = dict(heads=heads, nope=nope, rope=rope, kv_lora=kv_lora, q_lora=w_q_up.shape[1],
                m_heads=w_ob.shape[1] // dh, m_kvh=kvh, dh=dh, mla_scale=(nope + rope) ** -0.5)
    dims["n_mq"] = dims["m_heads"] * dh
    dims["n_kv"] = kvh * dh
    tp, ts = batch * seq, dec_batch * dec_seq
    assert seq % MOBA_BLOCK == 0 and past % MOBA_BLOCK == 0 and dec_seq <= 16
    assert tp % TOKEN_TILE == 0 and ts % TOKEN_TILE == 0 and n_pages % min(DECODE_PAGES, n_pages) == 0
    assert seq % min(MLA_K_TILE, seq) == 0 and min(MLA_K_TILE, seq) % min(MLA_Q_TOKENS, seq) == 0

    x = jnp.concatenate([x_prompt.reshape(tp, d), x_sample.reshape(ts, d)], axis=0)
    pos = jnp.concatenate([jnp.tile(jnp.arange(seq, dtype=jnp.int32), batch),
                           jnp.tile(past + jnp.arange(dec_seq, dtype=jnp.int32), dec_batch)])
    outs = [[] for _ in range(8)]
    for layer in range(depth):
        lw = dict(g_attn=g_attn[layer], w_in=w_in[layer], g_q=g_q[layer], w_q_up=w_q_up[layer], g_kv=g_kv[layer],
                  w_uk=w_uk[layer], w_uv=w_uv[layer], w_oa=w_oa[layer], w_ob=w_ob[layer], w_out=w_out[layer],
                  g_ffn=g_ffn[layer], w_router=w_router[layer], b_router=b_router[layer])
        moe_w = (w_gate_up, b_gate_up, w_down, b_down)
        x2, yb, pos4, w4, (ckv, kpe, mk, mv) = _layer(
            x, pos, batch, seq, dec_batch, dec_seq, (cache_ckv, cache_kpe, cache_k, cache_v), page_table,
            lw, moe_w, layer, dims)
        final = layer == depth - 1
        x = _combine_call(pos4, w4, x2, g_final.reshape(1, d), yb, final)
        outs[0].append(ckv[:tp].reshape(batch, seq, kv_lora))
        outs[1].append(kpe[:tp].reshape(batch, seq, rope))
        outs[2].append(mk[:tp].reshape(batch, seq, kvh, dh))
        outs[3].append(mv[:tp].reshape(batch, seq, kvh, dh))
        outs[4].append(ckv[tp:].reshape(dec_batch, dec_seq, kv_lora))
        outs[5].append(kpe[tp:].reshape(dec_batch, dec_seq, rope))
        outs[6].append(mk[tp:].reshape(dec_batch, dec_seq, kvh, dh))
        outs[7].append(mv[tp:].reshape(dec_batch, dec_seq, kvh, dh))
    y_prompt = x[:tp].reshape(batch, seq, d)
    y_sample = x[tp:].reshape(dec_batch, dec_seq, d)
    return (y_prompt, y_sample) + tuple(jnp.stack(o) for o in outs)
```

```python
import functools

import jax
import jax.numpy as jnp
from jax import lax
from jax.experimental import pallas as pl
from jax.experimental.pallas import tpu as pltpu

F32 = jnp.float32
BF16 = jnp.bfloat16

NORM_EPS = 1e-6
ROPE_THETA = 10000.0
MOBA_BLOCK = 256
MOBA_TOPK = 3
MOE_TOP_K = 4
SWIGLU_LIMIT = 7.0
SWIGLU_ALPHA = 1.702
NEG_INF = -1e30

LANES = 128
VMEM_LIMIT = 56 * 1024 * 1024

TOKEN_TILE = 256
MLA_Q_TOKENS = 128
MLA_K_TILE = 512
DECODE_PAGES = 16
MOBA_DECODE_PAGES = 32
EXPERT_ROWS = 768
EXPERT_SUB_ROWS = 256
EXPERT_FF = 256
COMBINE_TOKENS = 128


def _rms(x, g):
    return x * lax.rsqrt(jnp.mean(x * x, axis=-1, keepdims=True) + NORM_EPS) * g


def _dot(a, b):
    return jnp.dot(a, b, preferred_element_type=F32)


def _dot_nt(a, b):
    return lax.dot_general(a, b, (((1,), (1,)), ((), ())), preferred_element_type=F32)


def _topk_onehot(vals, k, valid, axis=-1):
    axis = axis % vals.ndim
    n = vals.shape[axis]
    iota = lax.broadcasted_iota(jnp.int32, vals.shape, axis).astype(F32)
    work = jnp.where(valid, vals, -jnp.inf)
    sel = jnp.zeros(vals.shape, F32)
    for _ in range(k):
        m = jnp.max(work, axis=axis, keepdims=True)
        idx = jnp.min(jnp.where(work == m, iota, float(n)), axis=axis, keepdims=True)
        hit = iota == idx
        sel = jnp.where(hit, 1.0, sel)
        work = jnp.where(hit, -jnp.inf, work)
    return jnp.where(valid, sel, 0.0)


def _params(sem, vmem=VMEM_LIMIT):
    return pltpu.CompilerParams(dimension_semantics=sem, vmem_limit_bytes=vmem)


def _const_spec(shape):
    nd = len(shape)
    return pl.BlockSpec(shape, lambda *_: (0,) * nd, pipeline_mode=pl.Buffered(1))


def _proj_body(x_ref, cos_ref, sin_ref, gat_ref, wa_ref, gq_ref, wq_ref, gkv_ref, wuk_ref,
               qabs_ref, qrope_ref, ckv32_ref, ckv16_ref, kpe32_ref, kpe16_ref,
               mq_ref, mk32_ref, mv32_ref, mk16_ref, mv16_ref, kmean_ref, ckvt_ref, mvt_ref,
               *, q_lora, kv_lora, rope, n_mq, n_kv, heads, nope, scale):
    hb = _rms(x_ref[...], gat_ref[...]).astype(BF16)
    off = [0]

    def proj(n):
        r = _dot(hb, wa_ref[:, off[0]:off[0] + n])
        off[0] += n
        return r

    q_lat = proj(q_lora)
    c_raw = proj(kv_lora)
    mq = proj(n_mq)
    mk = proj(n_kv)
    mv = proj(n_kv)
    kpe_a = proj(LANES)
    kpe_b = proj(LANES)

    cos = cos_ref[...]
    sin = sin_ref[...]
    ckv = _rms(c_raw, gkv_ref[...])
    ckv32_ref[...] = ckv
    ckv16_ref[...] = ckv.astype(BF16)
    kpe = kpe_a * cos + kpe_b * sin
    kpe32_ref[...] = kpe[:, :rope]
    kpe16_ref[...] = kpe.astype(BF16)
    mq_ref[...] = mq.astype(BF16)
    mk32_ref[...] = mk
    mv32_ref[...] = mv
    mk16_ref[...] = mk.astype(BF16)
    mv16_ref[...] = mv.astype(BF16)
    kmean_ref[...] = jnp.sum(mk, axis=0, keepdims=True) * (1.0 / MOBA_BLOCK)
    ckvt_ref[...] = ckv.T.astype(BF16)
    mvt_ref[...] = mv.T.astype(BF16)

    qn = _rms(q_lat, gq_ref[...]).astype(BF16)
    hn = heads * nope
    hr = heads * LANES
    q_nope = _dot(qn, wq_ref[:, :hn])
    q_ra = _dot(qn, wq_ref[:, hn:hn + hr])
    q_rb = _dot(qn, wq_ref[:, hn + hr:hn + 2 * hr])
    for h in range(heads):
        lanes = slice(h * LANES, (h + 1) * LANES)
        qrope_ref[:, lanes] = ((q_ra[:, lanes] * cos + q_rb[:, lanes] * sin) * scale).astype(BF16)
        qa = _dot(q_nope[:, h * nope:(h + 1) * nope].astype(BF16), wuk_ref[h])
        qabs_ref[:, h * kv_lora:(h + 1) * kv_lora] = (qa * scale).astype(BF16)


def _proj_call(x, cos_t, sin_t, g_attn, wa, g_q, wq, g_kv, wuk, dims):
    t, d = x.shape
    tm = TOKEN_TILE
    heads, nope, rope = dims["heads"], dims["nope"], dims["rope"]
    q_lora, kv_lora, n_mq, n_kv = dims["q_lora"], dims["kv_lora"], dims["n_mq"], dims["n_kv"]
    nt = t // tm
    row = lambda w: pl.BlockSpec((tm, w), lambda i: (i, 0))
    out_shape = (
        jax.ShapeDtypeStruct((t, heads * kv_lora), BF16),
        jax.ShapeDtypeStruct((t, heads * LANES), BF16),
        jax.ShapeDtypeStruct((t, kv_lora), F32),
        jax.ShapeDtypeStruct((t, kv_lora), BF16),
        jax.ShapeDtypeStruct((t, rope), F32),
        jax.ShapeDtypeStruct((t, LANES), BF16),
        jax.ShapeDtypeStruct((t, n_mq), BF16),
        jax.ShapeDtypeStruct((t, n_kv), F32),
        jax.ShapeDtypeStruct((t, n_kv), F32),
        jax.ShapeDtypeStruct((t, n_kv), BF16),
        jax.ShapeDtypeStruct((t, n_kv), BF16),
        jax.ShapeDtypeStruct((nt, 1, n_kv), F32),
        jax.ShapeDtypeStruct((nt, kv_lora, tm), BF16),
        jax.ShapeDtypeStruct((nt, n_kv, tm), BF16),
    )
    tile3 = lambda w: pl.BlockSpec((None, w, tm), lambda i: (i, 0, 0))
    out_specs = (row(heads * kv_lora), row(heads * LANES), row(kv_lora), row(kv_lora), row(rope), row(LANES),
                 row(n_mq), row(n_kv), row(n_kv), row(n_kv), row(n_kv),
                 pl.BlockSpec((None, 1, n_kv), lambda i: (i, 0, 0)), tile3(kv_lora), tile3(n_kv))
    in_specs = [row(d), row(LANES), row(LANES), _const_spec(g_attn.shape), _const_spec(wa.shape),
                _const_spec(g_q.shape), _const_spec(wq.shape), _const_spec(g_kv.shape), _const_spec(wuk.shape)]
    body = functools.partial(_proj_body, q_lora=q_lora, kv_lora=kv_lora, rope=rope, n_mq=n_mq, n_kv=n_kv,
                             heads=heads, nope=nope, scale=dims["mla_scale"])
    return pl.pallas_call(body, grid=(nt,), in_specs=in_specs, out_specs=out_specs, out_shape=out_shape,
                          compiler_params=_params(("arbitrary",)), name="proj")(
                              x, cos_t, sin_t, g_attn, wa, g_q, wq, g_kv, wuk)


def _mla_prompt_body(qa_ref, qr_ref, ckv_ref, kpe_ref, ckvt_ref, out_ref, m_scr, l_scr, acc_scr,
                     *, tq, tk, heads):
    qi = pl.program_id(1)
    kv_lora = ckv_ref.shape[1]
    sub = ckvt_ref.shape[2]
    qa = jnp.concatenate([qa_ref[:, h * kv_lora:(h + 1) * kv_lora] for h in range(heads)], axis=0)
    qr = jnp.concatenate([qr_ref[:, h * LANES:(h + 1) * LANES] for h in range(heads)], axis=0)
    m_scr[...] = jnp.full(m_scr.shape, NEG_INF, F32)
    l_scr[...] = jnp.zeros(l_scr.shape, F32)
    acc_scr[...] = jnp.zeros(acc_scr.shape, F32)
    tok = qi * tq + jnp.concatenate([lax.broadcasted_iota(jnp.int32, (1, tq), 1)] * heads, axis=1)
    n_k = lax.div(qi * tq, tk) + 1

    def step(kt, carry):
        start = pl.multiple_of(kt * tk, tk)
        kc = ckv_ref[pl.ds(start, tk), :]
        kr = kpe_ref[pl.ds(start, tk), :]
        s = _dot_nt(kc, qa) + _dot_nt(kr, qr)
        key = start + lax.broadcasted_iota(jnp.int32, (tk, 1), 0)
        s = jnp.where(key <= tok, s, NEG_INF)
        m_old = m_scr[...]
        m_new = jnp.maximum(m_old, jnp.max(s, axis=0, keepdims=True))
        alpha = jnp.exp(m_old - m_new)
        p = jnp.exp(s - m_new)
        l_scr[...] = alpha * l_scr[...] + jnp.sum(p, axis=0, keepdims=True)
        pb = p.astype(BF16)
        pv = _dot(ckvt_ref[kt * (tk // sub)], pb[:sub, :])
        for c in range(1, tk // sub):
            pv = pv + _dot(ckvt_ref[kt * (tk // sub) + c], pb[c * sub:(c + 1) * sub, :])
        acc_scr[...] = alpha * acc_scr[...] + pv
        m_scr[...] = m_new
        return carry

    lax.fori_loop(0, n_k, step, 0)
    o_t = acc_scr[...] / l_scr[...]
    for h in range(heads):
        out_ref[:, h * kv_lora:(h + 1) * kv_lora] = o_t[:, h * tq:(h + 1) * tq].T.astype(BF16)


def _mla_prompt_call(qabs, qrope, ckv16, kpe16, ckvt, batch, seq, dims):
    heads, kv_lora = dims["heads"], dims["kv_lora"]
    tq = min(MLA_Q_TOKENS, seq)
    tk = min(MLA_K_TILE, seq)
    nq = seq // tq
    rows = tq * heads
    sub = ckvt.shape[2]
    body = functools.partial(_mla_prompt_body, tq=tq, tk=tk, heads=heads)
    return pl.pallas_call(
        body, grid=(batch, nq),
        in_specs=[pl.BlockSpec((tq, heads * kv_lora), lambda b, q: (b * nq + q, 0)),
                  pl.BlockSpec((tq, heads * LANES), lambda b, q: (b * nq + q, 0)),
                  pl.BlockSpec((seq, kv_lora), lambda b, q: (b, 0)),
                  pl.BlockSpec((seq, LANES), lambda b, q: (b, 0)),
                  pl.BlockSpec((seq // sub, kv_lora, sub), lambda b, q: (b, 0, 0))],
        out_specs=pl.BlockSpec((tq, heads * kv_lora), lambda b, q: (b * nq + q, 0)),
        out_shape=jax.ShapeDtypeStruct((batch * seq, heads * kv_lora), BF16),
        scratch_shapes=[pltpu.VMEM((1, rows), F32), pltpu.VMEM((1, rows), F32), pltpu.VMEM((kv_lora, rows), F32)],
        compiler_params=_params(("arbitrary", "arbitrary")), name="mla_prompt")(qabs, qrope, ckv16, kpe16, ckvt)


def _moba_prompt_body(q_ref, k_ref, vt_ref, kmean_ref, slope_ref, out_ref, sel_scr, m_scr, l_scr, acc_scr,
                      *, gh, dh, nb, scale):
    qi = pl.program_id(2)
    blk = MOBA_BLOCK
    rows = gh * blk
    q = jnp.concatenate([q_ref[:, j * dh:(j + 1) * dh] for j in range(gh)], axis=0)
    slope = slope_ref[...]
    tpos = lax.broadcasted_iota(jnp.int32, (1, blk), 1)
    qpos = qi * blk + jnp.concatenate([tpos] * gh, axis=1)
    qposf = qpos.astype(F32)

    gate = _dot_nt(kmean_ref[...].astype(BF16), q)
    biota = lax.broadcasted_iota(jnp.int32, (nb, rows), 0)
    sel_scr[...] = _topk_onehot(gate, MOBA_TOPK, biota < qi, axis=0)

    def scores(kt):
        start = pl.multiple_of(kt * blk, blk)
        kb = k_ref[pl.ds(start, blk), :]
        kpos = start + lax.broadcasted_iota(jnp.int32, (blk, 1), 0)
        s = _dot_nt(kb, q) * scale - slope * (qposf - kpos.astype(F32))
        return s, vt_ref[kt], kpos

    s, vt, kpos = scores(qi)
    s = jnp.where(kpos <= qpos, s, NEG_INF)
    m0 = jnp.max(s, axis=0, keepdims=True)
    p = jnp.exp(s - m0)
    m_scr[...] = m0
    l_scr[...] = jnp.sum(p, axis=0, keepdims=True)
    acc_scr[...] = _dot(vt, p.astype(BF16))

    def step(kt, carry):
        s, vt, _ = scores(kt)
        chosen = sel_scr[pl.ds(kt, 1), :]
        s = jnp.where(chosen > 0.5, s, NEG_INF)
        m_old = m_scr[...]
        m_new = jnp.maximum(m_old, jnp.max(s, axis=0, keepdims=True))
        alpha = jnp.exp(m_old - m_new)
        p = jnp.exp(s - m_new)
        l_scr[...] = alpha * l_scr[...] + jnp.sum(p, axis=0, keepdims=True)
        acc_scr[...] = alpha * acc_scr[...] + _dot(vt, p.astype(BF16))
        m_scr[...] = m_new
        return carry

    lax.fori_loop(0, qi, step, 0)
    o_t = acc_scr[...] / l_scr[...]
    for j in range(gh):
        out_ref[:, j * dh:(j + 1) * dh] = o_t[:, j * blk:(j + 1) * blk].T.astype(BF16)


def _moba_prompt_call(mq16, mk16, mvt, kmean, slope_rows, batch, seq, dims):
    heads, kvh, dh = dims["m_heads"], dims["m_kvh"], dims["dh"]
    gh = heads // kvh
    blk = MOBA_BLOCK
    nb = seq // blk
    rows = gh * blk
    body = functools.partial(_moba_prompt_body, gh=gh, dh=dh, nb=nb, scale=dh ** -0.5)
    return pl.pallas_call(
        body, grid=(batch, kvh, nb),
        in_specs=[pl.BlockSpec((blk, gh * dh), lambda b, g, q: (b * nb + q, g)),
                  pl.BlockSpec((seq, dh), lambda b, g, q: (b, g)),
                  pl.BlockSpec((nb, dh, blk), lambda b, g, q: (b, g, 0)),
                  pl.BlockSpec((nb, dh), lambda b, g, q: (b, g)),
                  pl.BlockSpec((None, 1, rows), lambda b, g, q: (g, 0, 0))],
        out_specs=pl.BlockSpec((blk, gh * dh), lambda b, g, q: (b * nb + q, g)),
        out_shape=jax.ShapeDtypeStruct((batch * seq, heads * dh), BF16),
        scratch_shapes=[pltpu.VMEM((nb, rows), F32), pltpu.VMEM((1, rows), F32), pltpu.VMEM((1, rows), F32),
                        pltpu.VMEM((dh, rows), F32)],
        compiler_params=_params(("arbitrary", "arbitrary", "arbitrary")), name="moba_prompt")(
            mq16, mk16, mvt, kmean, slope_rows)


def _page_spec(block, layer, pages_per_step, i, n_steps=None, phase=0):
    nd = len(block)

    def index(s, j, pt):
        jj = j - phase
        if n_steps is not None:
            jj = jnp.clip(jj, 0, n_steps - 1)
        return (layer, pt[s, jj * pages_per_step + i]) + (0,) * (nd - 2)

    return pl.BlockSpec(block, index)


def _mla_decode_body(pt_ref, qa_ref, qr_ref, cnew_ref, rnew_ref, *rest, pages, heads, n_new):
    ckv_pages = rest[:pages]
    kpe_pages = rest[pages:2 * pages]
    out_ref, m_scr, l_scr, acc_scr = rest[2 * pages:]
    j = pl.program_id(1)
    qa = qa_ref[...]
    qr = qr_ref[...]
    rows = qa.shape[0]

    @pl.when(j == 0)
    def _():
        cn = cnew_ref[...]
        s = _dot_nt(qa, cn) + _dot_nt(qr, rnew_ref[...])
        tok = lax.div(lax.broadcasted_iota(jnp.int32, (rows, 1), 0), heads)
        key = lax.broadcasted_iota(jnp.int32, (1, cn.shape[0]), 1)
        s = jnp.where((key <= tok) & (key < n_new), s, NEG_INF)
        m0 = jnp.max(s, axis=-1, keepdims=True)
        p = jnp.exp(s - m0)
        m_scr[...] = m0
        l_scr[...] = jnp.sum(p, axis=-1, keepdims=True)
        acc_scr[...] = _dot(p.astype(BF16), cn)

    kc = jnp.concatenate([r[...].astype(BF16) for r in ckv_pages], axis=0)
    kr_t = jnp.concatenate([r[...].astype(BF16) for r in kpe_pages], axis=1)
    s = _dot_nt(qa, kc) + _dot(qr[:, :kr_t.shape[0]], kr_t)
    m_old = m_scr[...]
    m_new = jnp.maximum(m_old, jnp.max(s, axis=-1, keepdims=True))
    alpha = jnp.exp(m_old - m_new)
    p = jnp.exp(s - m_new)
    l_scr[...] = alpha * l_scr[...] + jnp.sum(p, axis=-1, keepdims=True)
    acc_scr[...] = alpha * acc_scr[...] + _dot(p.astype(BF16), kc)
    m_scr[...] = m_new

    @pl.when(j == pl.num_programs(1) - 1)
    def _():
        out_ref[...] = acc_scr[...] / l_scr[...]


def _mla_decode_call(page_table, qa_s, qr_s, cnew, rnew, cache_ckv, cache_kpe, layer, dims, n_new):
    ns, rows, kv_lora = qa_s.shape
    n_pages = page_table.shape[1]
    page = cache_ckv.shape[2]
    rope = cache_kpe.shape[3]
    kpe_t = jnp.swapaxes(cache_kpe, 2, 3)
    pages = min(DECODE_PAGES, n_pages)
    nj = n_pages // pages
    npad = cnew.shape[1]
    seq_spec = lambda r, w: pl.BlockSpec((None, r, w), lambda s, j, pt: (s, 0, 0))
    in_specs = [seq_spec(rows, kv_lora), seq_spec(rows, LANES), seq_spec(npad, kv_lora), seq_spec(npad, LANES)]
    in_specs += [_page_spec((None, None, page, kv_lora), layer, pages, i) for i in range(pages)]
    in_specs += [_page_spec((None, None, rope, page), layer, pages, i) for i in range(pages)]
    body = functools.partial(_mla_decode_body, pages=pages, heads=dims["heads"], n_new=n_new)
    grid_spec = pltpu.PrefetchScalarGridSpec(
        num_scalar_prefetch=1, grid=(ns, nj), in_specs=in_specs, out_specs=seq_spec(rows, kv_lora),
        scratch_shapes=[pltpu.VMEM((rows, 1), F32), pltpu.VMEM((rows, 1), F32), pltpu.VMEM((rows, kv_lora), F32)])
    return pl.pallas_call(body, grid_spec=grid_spec, out_shape=jax.ShapeDtypeStruct((ns, rows, kv_lora), F32),
                          compiler_params=_params(("arbitrary", "arbitrary")), name="mla_decode")(
                              page_table, qa_s, qr_s, cnew, rnew, *([cache_ckv] * pages), *([kpe_t] * pages))


def _moba_decode_body(pt_ref, q_ref, knew_ref, vnew_ref, slope_ref, *rest,
                      pages, page, kvh, dh, n_new, past, scale, n_steps):
    k_pages = rest[:pages]
    v_pages = rest[pages:2 * pages]
    out_ref, s_scr, p_scr, mean_scr, l_scr, acc_scr = rest[2 * pages:]
    j = pl.program_id(1)
    nj = n_steps
    blk = MOBA_BLOCK
    keys = pages * page
    bps = keys // blk
    nblk = mean_scr.shape[1]
    rows = q_ref.shape[1]
    tok = lax.rem(lax.broadcasted_iota(jnp.int32, (rows, 1), 0), n_new)
    qposf = (past + tok).astype(F32)

    def head_rows(page_ref, g):
        return page_ref[pl.ds(g, page, stride=kvh), :]

    @pl.when(j < nj)
    def _():
        for g in range(kvh):
            kg = jnp.concatenate([head_rows(r, g) for r in k_pages], axis=0)
            sums = jnp.sum(kg.reshape(bps, blk, dh), axis=1)
            mean_scr[g, pl.ds(pl.multiple_of(j * bps, bps), bps), :] = sums * (1.0 / blk)
            s_scr[g, j] = _dot_nt(q_ref[g], kg.astype(BF16))

    @pl.when(j == nj)
    def _():
        ci = lax.broadcasted_iota(jnp.int32, (nblk, keys), 1)
        for g in range(kvh):
            q = q_ref[g]
            slope = slope_ref[g]
            gate = _dot_nt(q, mean_scr[g].astype(BF16))
            sel = _topk_onehot(gate, MOBA_TOPK, jnp.full(gate.shape, True)).astype(BF16)
            kn = knew_ref[:, g * dh:(g + 1) * dh]
            key = lax.broadcasted_iota(jnp.int32, (1, kn.shape[0]), 1)
            s_new = _dot_nt(q, kn) * scale - slope * (tok - key).astype(F32)
            s_new = jnp.where((key <= tok) & (key < n_new), s_new, NEG_INF)
            m = jnp.max(s_new, axis=-1, keepdims=True)
            for jj in range(n_steps):
                lo = (lax.broadcasted_iota(jnp.int32, (nblk, keys), 0) - jj * bps) * blk
                expand = jnp.where((ci >= lo) & (ci < lo + blk), 1.0, 0.0).astype(BF16)
                chosen = _dot(sel, expand)
                kpos = jj * keys + lax.broadcasted_iota(jnp.int32, (1, keys), 1)
                s = s_scr[g, jj] * scale - slope * (qposf - kpos.astype(F32))
                s = jnp.where(chosen > 0.5, s, NEG_INF)
                s_scr[g, jj] = s
                m = jnp.maximum(m, jnp.max(s, axis=-1, keepdims=True))
            p_new = jnp.exp(s_new - m)
            l = jnp.sum(p_new, axis=-1, keepdims=True)
            for jj in range(n_steps):
                p = jnp.exp(s_scr[g, jj] - m)
                l = l + jnp.sum(p, axis=-1, keepdims=True)
                p_scr[g, jj] = p.astype(BF16)
            l_scr[g] = l
            acc_scr[g] = _dot(p_new.astype(BF16), vnew_ref[:, g * dh:(g + 1) * dh])

    @pl.when(j >= nj)
    def _():
        jj = j - nj
        for g in range(kvh):
            vg = jnp.concatenate([head_rows(r, g).astype(BF16) for r in v_pages], axis=0)
            acc_scr[g] += _dot(p_scr[g, jj], vg)

    @pl.when(j == 2 * nj - 1)
    def _():
        for g in range(kvh):
            out_ref[g] = acc_scr[g] / l_scr[g]


def _moba_decode_call(page_table, q_s, knew, vnew, slope_rows, cache_k, cache_v, layer, n_new):
    ns, kvh, rows, dh = q_s.shape
    n_pages = page_table.shape[1]
    page = cache_k.shape[2]
    width = kvh * dh
    cache_k = cache_k.reshape(cache_k.shape[:2] + (page * kvh, dh))
    cache_v = cache_v.reshape(cache_v.shape[:2] + (page * kvh, dh))
    pages = min(MOBA_DECODE_PAGES, n_pages)
    nj = n_pages // pages
    past = n_pages * page
    nblk = past // MOBA_BLOCK
    keys = pages * page
    npad = knew.shape[1]
    in_specs = [pl.BlockSpec((None, kvh, rows, dh), lambda s, j, pt: (s, 0, 0, 0)),
                pl.BlockSpec((None, npad, width), lambda s, j, pt: (s, 0, 0)),
                pl.BlockSpec((None, npad, width), lambda s, j, pt: (s, 0, 0)),
                pl.BlockSpec((kvh, rows, 1), lambda s, j, pt: (0, 0, 0))]
    rows_pp = page * kvh
    in_specs += [_page_spec((None, None, rows_pp, dh), layer, pages, i, n_steps=nj, phase=0) for i in range(pages)]
    in_specs += [_page_spec((None, None, rows_pp, dh), layer, pages, i, n_steps=nj, phase=nj) for i in range(pages)]
    body = functools.partial(_moba_decode_body, pages=pages, page=page, kvh=kvh, dh=dh, n_new=n_new, past=past,
                             scale=dh ** -0.5, n_steps=nj)
    grid_spec = pltpu.PrefetchScalarGridSpec(
        num_scalar_prefetch=1, grid=(ns, 2 * nj), in_specs=in_specs,
        out_specs=pl.BlockSpec((None, kvh, rows, dh), lambda s, j, pt: (s, 0, 0, 0)),
        scratch_shapes=[pltpu.VMEM((kvh, nj, rows, keys), F32), pltpu.VMEM((kvh, nj, rows, keys), BF16),
                        pltpu.VMEM((kvh, nblk, dh), F32), pltpu.VMEM((kvh, rows, 1), F32),
                        pltpu.VMEM((kvh, rows, dh), F32)])
    return pl.pallas_call(body, grid_spec=grid_spec, out_shape=jax.ShapeDtypeStruct((ns, kvh, rows, dh), F32),
                          compiler_params=_params(("arbitrary", "arbitrary")), name="moba_decode")(
                              page_table, q_s, knew, vnew, slope_rows, *([cache_k] * pages), *([cache_v] * pages))


def _merge1_body(x_ref, latp_ref, lats_ref, mop_ref, mos_ref, gat_ref, wg_ref, wuv_ref, woa_ref, wob_ref, out_ref,
                 *, heads, kv_lora, prompt_tiles):
    d = x_ref.shape[1]
    is_prompt = pl.program_id(0) < prompt_tiles
    lat = jnp.where(is_prompt, latp_ref[...], lats_ref[...])
    mo = jnp.where(is_prompt, mop_ref[...], mos_ref[...])
    hb = _rms(x_ref[...], gat_ref[...]).astype(BF16)
    ga = _dot(hb, wg_ref[:, :d])
    gb = _dot(hb, wg_ref[:, d:])
    a_in = jnp.concatenate(
        [_dot(lat[:, h * kv_lora:(h + 1) * kv_lora], wuv_ref[h]).astype(BF16) for h in range(heads)], axis=1)
    a = _dot(a_in, woa_ref[...])
    m = _dot(mo, wob_ref[...])
    out_ref[...] = (jax.nn.sigmoid(ga) * a + jax.nn.sigmoid(gb) * m).astype(BF16)


def _merge1_call(x, lat_p, lat_s, mo_p, mo_s, g_attn, wg, wuv, woa, wob, dims):
    t, d = x.shape
    tm = TOKEN_TILE
    npt = lat_p.shape[0] // tm
    row = lambda w: pl.BlockSpec((tm, w), lambda i: (i, 0))
    prow = lambda w: pl.BlockSpec((tm, w), lambda i: (jnp.minimum(i, npt - 1), 0))
    srow = lambda w: pl.BlockSpec((tm, w), lambda i: (jnp.maximum(i - npt, 0), 0))
    body = functools.partial(_merge1_body, heads=dims["heads"], kv_lora=dims["kv_lora"], prompt_tiles=npt)
    return pl.pallas_call(
        body, grid=(t // tm,),
        in_specs=[row(d), prow(lat_p.shape[1]), srow(lat_s.shape[1]), prow(mo_p.shape[1]), srow(mo_s.shape[1]),
                  _const_spec(g_attn.shape), _const_spec(wg.shape),
                  _const_spec(wuv.shape), _const_spec(woa.shape), _const_spec(wob.shape)],
        out_specs=row(d), out_shape=jax.ShapeDtypeStruct((t, d), BF16),
        compiler_params=_params(("arbitrary",)), name="merge1")(
            x, lat_p, lat_s, mo_p, mo_s, g_attn, wg, wuv, woa, wob)


def _merge2_body(gated_ref, x_ref, wout_ref, gffn_ref, wr_hi_ref, wr_lo_ref, br_ref, x2_ref, h2_ref, selw_ref,
                 *, n_experts):
    x2 = x_ref[...] + _dot(gated_ref[...], wout_ref[...])
    x2_ref[...] = x2
    h2 = _rms(x2, gffn_ref[...])
    h2_ref[...] = h2
    h_hi = h2.astype(BF16)
    h_lo = (h2 - h_hi.astype(F32)).astype(BF16)
    logits = _dot(h_hi, wr_hi_ref[...]) + _dot(h_hi, wr_lo_ref[...]) + _dot(h_lo, wr_hi_ref[...]) + br_ref[...]
    valid = lax.broadcasted_iota(jnp.int32, logits.shape, 1) < n_experts
    sel = _topk_onehot(logits, MOE_TOP_K, valid)
    top = jnp.max(jnp.where(valid, logits, -jnp.inf), axis=-1, keepdims=True)
    e = jnp.where(sel > 0.0, jnp.exp(logits - top), 0.0)
    w = e / jnp.sum(e, axis=-1, keepdims=True)
    selw_ref[...] = jnp.where(sel > 0.0, w, -1.0)


def _merge2_call(gated, x, wout, g_ffn, wr_hi, wr_lo, br, n_experts):
    t, d = x.shape
    tm = TOKEN_TILE
    row = lambda w: pl.BlockSpec((tm, w), lambda i: (i, 0))
    body = functools.partial(_merge2_body, n_experts=n_experts)
    return pl.pallas_call(
        body, grid=(t // tm,),
        in_specs=[row(d), row(d), _const_spec(wout.shape), _const_spec(g_ffn.shape), _const_spec(wr_hi.shape),
                  _const_spec(wr_lo.shape), _const_spec(br.shape)],
        out_specs=(row(d), row(d), row(LANES)),
        out_shape=(jax.ShapeDtypeStruct((t, d), F32), jax.ShapeDtypeStruct((t, d), F32),
                   jax.ShapeDtypeStruct((t, LANES), F32)),
        compiler_params=_params(("arbitrary",)), name="merge2")(gated, x, wout, g_ffn, wr_hi, wr_lo, br)


def _row_copy(src_hbm, src_row, dst, dst_row, sem):
    return pltpu.make_async_copy(src_hbm.at[pl.ds(src_row, 1), :], dst.at[pl.ds(dst_row, 1), :], sem)


def _tile_copy(src_hbm, dst, sem):
    return pltpu.make_async_copy(src_hbm.at[pl.ds(0, dst.shape[0]), :], dst, sem)


def _expert_body(te_ref, nu_ref, tb_ref, tv_ref, cur_a, cur_b, nxt_a, nxt_b, h_hbm,
                 wg_ref, wu_ref, bg_ref, bu_ref, wd_ref, bd_ref, out_ref, xg_scr, xb_scr, sem, *, sub_rows):
    i = pl.program_id(0)
    f = pl.program_id(1)
    n_used = nu_ref[0]
    tm = xb_scr.shape[0]

    def gather(tile_idx, blk_a, blk_b):
        off = lax.rem(tb_ref[tile_idx], tm)
        valid = tv_ref[tile_idx]

        def one(r, carry):
            idx = off + r
            tok = jnp.where(idx < tm, blk_a[0, jnp.minimum(idx, tm - 1)], blk_b[0, jnp.maximum(idx - tm, 0)])
            tok = jnp.where(r < valid, tok, 0)
            _row_copy(h_hbm, tok, xg_scr, r, sem).start()
            return carry

        lax.fori_loop(0, tm, one, 0, unroll=8)

    @pl.when((f == 0) & (i == 0))
    def _():
        gather(0, cur_a, cur_b)

    @pl.when((f == 0) & (i < n_used))
    def _():
        _tile_copy(h_hbm, xg_scr, sem).wait()
        xb_scr[...] = xg_scr[...].astype(BF16)
        out_ref[...] = jnp.broadcast_to(bd_ref[...], out_ref.shape)

    @pl.when((f == 0) & (i + 1 < n_used))
    def _():
        gather(i + 1, nxt_a, nxt_b)

    @pl.when((f == 0) & (i >= n_used))
    def _():
        out_ref[...] = jnp.zeros(out_ref.shape, F32)

    @pl.when(i < n_used)
    def _():
        wg = wg_ref[...].astype(BF16)
        wu = wu_ref[...].astype(BF16)
        wd = wd_ref[...].astype(BF16)
        bg = bg_ref[...]
        bu = bu_ref[...]
        for r0 in range(0, tm, sub_rows):
            rows = slice(r0, r0 + sub_rows)
            xb = xb_scr[rows, :]
            gate = jnp.minimum(_dot(xb, wg) + bg, SWIGLU_LIMIT)
            up = jnp.clip(_dot(xb, wu) + bu, -SWIGLU_LIMIT, SWIGLU_LIMIT)
            act = (up + 1.0) * gate * jax.nn.sigmoid(SWIGLU_ALPHA * gate)
            out_ref[rows, :] += _dot(act.astype(BF16), wd)


def _expert_call(route, h2, moe_w, layer):
    tile_expert, n_used, tile_base, tile_valid, sorted_blocks = route
    w_gate_up, b_gate_up, w_down, b_down = moe_w
    d = h2.shape[1]
    depth, n_exp, _, two_f = w_gate_up.shape
    ff = two_f // 2
    tm = EXPERT_ROWS
    tf = min(EXPERT_FF, ff)
    nf = ff // tf
    nt = tile_expert.shape[0]
    nblk = sorted_blocks.shape[0]
    bgu = b_gate_up.reshape(depth, n_exp, 1, two_f)
    bd = b_down.reshape(depth, n_exp, 1, d)

    def tile(i, nu):
        return jnp.minimum(i, nu[0] - 1)

    def expert(i, te, nu):
        return te[tile(i, nu)]

    def col(i, f, nu):
        return jnp.where(i < nu[0], f, nf - 1)

    def tok_block(ahead, second):
        def index(i, f, te, nu, tb, tv):
            blk = lax.div(tb[tile(i + ahead, nu)], tm) + second
            return (jnp.minimum(blk, nblk - 1), 0, 0)
        return pl.BlockSpec((None, 1, tm), index, memory_space=pltpu.SMEM)

    wspec = lambda shape, index: pl.BlockSpec((None, None) + shape, index)
    in_specs = [
        tok_block(0, 0), tok_block(0, 1), tok_block(1, 0), tok_block(1, 1),
        pl.BlockSpec(memory_space=pl.ANY),
        wspec((d, tf), lambda i, f, te, nu, tb, tv: (layer, expert(i, te, nu), 0, col(i, f, nu))),
        wspec((d, tf), lambda i, f, te, nu, tb, tv: (layer, expert(i, te, nu), 0, nf + col(i, f, nu))),
        wspec((1, tf), lambda i, f, te, nu, tb, tv: (layer, expert(i, te, nu), 0, col(i, f, nu))),
        wspec((1, tf), lambda i, f, te, nu, tb, tv: (layer, expert(i, te, nu), 0, nf + col(i, f, nu))),
        wspec((tf, d), lambda i, f, te, nu, tb, tv: (layer, expert(i, te, nu), col(i, f, nu), 0)),
        wspec((1, d), lambda i, f, te, nu, tb, tv: (layer, expert(i, te, nu), 0, 0)),
    ]
    grid_spec = pltpu.PrefetchScalarGridSpec(
        num_scalar_prefetch=4, grid=(nt, nf), in_specs=in_specs,
        out_specs=pl.BlockSpec((tm, d), lambda i, f, te, nu, tb, tv: (i, 0)),
        scratch_shapes=[pltpu.VMEM((tm, d), F32), pltpu.VMEM((tm, d), BF16), pltpu.SemaphoreType.DMA(())])
    body = functools.partial(_expert_body, sub_rows=min(EXPERT_SUB_ROWS, tm))
    return pl.pallas_call(
        body, grid_spec=grid_spec, out_shape=jax.ShapeDtypeStruct((nt * tm, d), F32),
        compiler_params=_params(("arbitrary", "arbitrary")), name="moe_experts")(
            tile_expert, n_used, tile_base, tile_valid, sorted_blocks, sorted_blocks, sorted_blocks, sorted_blocks,
            h2, w_gate_up, w_gate_up, bgu, bgu, w_down, bd)


def _combine_body(pos_cur_ref, pos_next_ref, w_ref, x2_ref, gf_ref, yb_hbm, y_ref, buf, sem, *, tm, top_k, final):
    i = pl.program_id(0)
    n = pl.num_programs(0)
    slot = lax.rem(i, 2)

    def issue(pos_ref, s):
        for k in range(top_k):
            def one(r, carry, k=k):
                _row_copy(yb_hbm, pos_ref[0, k * tm + r], buf.at[s, k], r, sem.at[s]).start()
                return carry
            lax.fori_loop(0, tm, one, 0, unroll=8)

    @pl.when(i == 0)
    def _():
        issue(pos_cur_ref, 0)

    @pl.when(i + 1 < n)
    def _():
        issue(pos_next_ref, 1 - slot)

    for k in range(top_k):
        _tile_copy(yb_hbm, buf.at[slot, k], sem.at[slot]).wait()

    w = w_ref[...]
    moe = w[:, 0:1] * buf[slot, 0]
    for k in range(1, top_k):
        moe = moe + w[:, k:k + 1] * buf[slot, k]
    y = x2_ref[...] + moe
    y_ref[...] = _rms(y, gf_ref[...]) if final else y


def _combine_call(pos4, w4, x2, g_final, yb, final):
    t, d = x2.shape
    top_k = pos4.shape[1]
    tm = COMBINE_TOKENS
    nt = t // tm
    pos_tiles = pos4.reshape(nt, tm, top_k).transpose(0, 2, 1).reshape(nt, 1, top_k * tm)
    body = functools.partial(_combine_body, tm=tm, top_k=top_k, final=final)
    smem = lambda index: pl.BlockSpec((None, 1, top_k * tm), index, memory_space=pltpu.SMEM)
    return pl.pallas_call(
        body, grid=(nt,),
        in_specs=[smem(lambda i: (i, 0, 0)), smem(lambda i: (jnp.minimum(i + 1, nt - 1), 0, 0)),
                  pl.BlockSpec((tm, top_k), lambda i: (i, 0)), pl.BlockSpec((tm, d), lambda i: (i, 0)),
                  _const_spec(g_final.shape), pl.BlockSpec(memory_space=pl.ANY)],
        out_specs=pl.BlockSpec((tm, d), lambda i: (i, 0)),
        out_shape=jax.ShapeDtypeStruct((t, d), F32),
        scratch_shapes=[pltpu.VMEM((2, top_k, tm, d), F32), pltpu.SemaphoreType.DMA((2,))],
        compiler_params=_params(("arbitrary",)), name="moe_combine")(pos_tiles, pos_tiles, w4, x2, g_final, yb)


def _route(selw, n_experts, tile_rows):
    t = selw.shape[0]
    i32 = jnp.int32
    chosen = selw >= 0.0
    sel = chosen.astype(i32)
    counts = jnp.sum(sel, axis=0)
    tiles_per = (counts + tile_rows - 1) // tile_rows
    tile_end = jnp.cumsum(tiles_per)
    tile_start = tile_end - tiles_per
    n_used = tile_end[-1]
    starts = jnp.cumsum(counts) - counts
    n_pairs = t * MOE_TOP_K
    n_tiles = -(-n_pairs // tile_rows) + n_experts
    ti = jnp.arange(n_tiles, dtype=i32)
    tile_expert = jnp.minimum(jnp.sum((ti[:, None] >= tile_end[None, :]).astype(i32), axis=1), n_experts - 1)
    is_e = tile_expert[:, None] == jnp.arange(n_experts, dtype=i32)[None, :]
    pick = lambda v: jnp.sum(jnp.where(is_e, v[None, :], 0), axis=1)
    tile_in_e = ti - pick(tile_start)
    tile_base = pick(starts) + tile_in_e * tile_rows
    tile_valid = jnp.clip(pick(counts) - tile_in_e * tile_rows, 0, tile_rows)
    pos = (tile_start * tile_rows)[None, :] + jnp.cumsum(sel, axis=0) - 1
    kth = jnp.cumsum(sel, axis=1)
    e_iota = jnp.arange(n_experts, dtype=i32)[None, :]
    pos4, w4, e4 = [], [], []
    for k in range(MOE_TOP_K):
        m = chosen & (kth == k + 1)
        pos4.append(jnp.sum(jnp.where(m, pos, 0), axis=1))
        w4.append(jnp.sum(jnp.where(m, selw, 0.0), axis=1))
        e4.append(jnp.sum(jnp.where(m, e_iota, 0), axis=1))
    pos4, w4, e4 = (jnp.stack(a, axis=1) for a in (pos4, w4, e4))
    order = jnp.argsort(e4.reshape(-1))
    sorted_tok = (order // MOE_TOP_K).astype(i32)
    n_blocks = -(-n_pairs // tile_rows) + 1
    sorted_blocks = jnp.pad(sorted_tok, (0, n_blocks * tile_rows - n_pairs)).reshape(n_blocks, 1, tile_rows)
    route = (tile_expert.astype(i32), n_used.reshape(1).astype(i32), tile_base.astype(i32),
             tile_valid.astype(i32), sorted_blocks)
    return route, pos4.astype(i32), w4


def _rope_tables(pos, rope):
    half = rope // 2
    inv = ROPE_THETA ** (-jnp.arange(half, dtype=F32) / half)
    ang = pos.astype(F32)[:, None] * inv[None, :]
    cos, sin = jnp.cos(ang), jnp.sin(ang)
    zero = jnp.zeros((pos.shape[0], LANES - rope), F32)
    return jnp.concatenate([cos, cos, zero], axis=1), jnp.concatenate([-sin, sin, zero], axis=1)


def _pad_lanes(w):
    return jnp.pad(w, [(0, 0)] * (w.ndim - 1) + [(0, LANES - w.shape[-1])])


def _swap_halves(w, rope):
    half = rope // 2
    return jnp.concatenate([w[..., half:], w[..., :half]], axis=-1)


def _layer(x, pos, batch, seq, dec_batch, dec_seq, caches, page_table, lw, moe_w, layer, dims):
    cache_ckv, cache_kpe, cache_k, cache_v = caches
    t, d = x.shape
    heads, nope, rope, kv_lora = dims["heads"], dims["nope"], dims["rope"], dims["kv_lora"]
    m_heads, kvh, dh = dims["m_heads"], dims["m_kvh"], dims["dh"]
    tp = batch * seq

    w_in = lw["w_in"]
    sizes = (dims["q_lora"], kv_lora, rope, m_heads * dh, kvh * dh, kvh * dh, d, d)
    offs = [0]
    for n in sizes:
        offs.append(offs[-1] + n)
    col = lambda k: w_in[:, offs[k]:offs[k + 1]]
    wa = jnp.concatenate([col(0), col(1), col(3), col(4), col(5), _pad_lanes(col(2)),
                          _pad_lanes(_swap_halves(col(2), rope))], axis=1).astype(BF16)
    wg = jnp.concatenate([col(6), col(7)], axis=1).astype(BF16)
    wq3 = lw["w_q_up"].reshape(dims["q_lora"], heads, nope + rope)
    wq_rope = wq3[:, :, nope:]
    wq = jnp.concatenate([wq3[:, :, :nope].reshape(-1, heads * nope),
                          _pad_lanes(wq_rope).reshape(-1, heads * LANES),
                          _pad_lanes(_swap_halves(wq_rope, rope)).reshape(-1, heads * LANES)], axis=1).astype(BF16)
    wuk = jnp.transpose(lw["w_uk"], (1, 2, 0)).astype(BF16)
    wuv = jnp.transpose(lw["w_uv"], (1, 0, 2)).astype(BF16)
    woa = lw["w_oa"].astype(BF16)
    wob = lw["w_ob"].astype(BF16)
    wout = lw["w_out"].astype(BF16)
    n_experts = lw["w_router"].shape[1]
    wr = jnp.pad(lw["w_router"], ((0, 0), (0, LANES - n_experts)))
    wr_hi = wr.astype(BF16)
    wr_lo = (wr - wr_hi.astype(F32)).astype(BF16)
    br = jnp.pad(lw["b_router"], (0, LANES - n_experts)).reshape(1, LANES)
    row1 = lambda g: g.reshape(1, -1)

    cos_t, sin_t = _rope_tables(pos, rope)
    (qabs, qrope, ckv32, ckv16, kpe32, kpe16, mq16, mk32, mv32, mk16, mv16, kmean, ckvt, mvt) = _proj_call(
        x, cos_t, sin_t, row1(lw["g_attn"]), wa, row1(lw["g_q"]), wq, row1(lw["g_kv"]), wuk, dims)
    kmean = kmean.reshape(t // TOKEN_TILE, kvh * dh)

    gh = m_heads // kvh
    slopes = 2.0 ** (-(8.0 / m_heads) * jnp.arange(1, m_heads + 1, dtype=F32))
    slope_p = jnp.repeat(slopes.reshape(kvh, gh), MOBA_BLOCK, axis=1).reshape(kvh, 1, gh * MOBA_BLOCK)
    lat_p = _mla_prompt_call(qabs, qrope, ckv16, kpe16, ckvt, batch, seq, dims)
    mo_p = _moba_prompt_call(mq16, mk16, mvt, kmean, slope_p, batch, seq, dims)

    npad = 16
    pad_new = lambda a: jnp.pad(a[tp:].reshape(dec_batch, dec_seq, -1), ((0, 0), (0, npad - dec_seq), (0, 0)))
    qa_s = qabs[tp:].reshape(dec_batch, dec_seq * heads, kv_lora)
    qr_s = qrope[tp:].reshape(dec_batch, dec_seq * heads, LANES)
    lat_s = _mla_decode_call(page_table, qa_s, qr_s, pad_new(ckv16), pad_new(kpe16), cache_ckv, cache_kpe,
                             layer, dims, dec_seq)
    q_m = mq16[tp:].reshape(dec_batch, dec_seq, kvh, gh, dh).transpose(0, 2, 3, 1, 4)
    q_m = q_m.reshape(dec_batch, kvh, gh * dec_seq, dh)
    slope_s = jnp.repeat(slopes.reshape(kvh, gh), dec_seq, axis=1).reshape(kvh, gh * dec_seq, 1)
    mo_s = _moba_decode_call(page_table, q_m, pad_new(mk16), pad_new(mv16), slope_s, cache_k, cache_v,
                             layer, dec_seq)
    mo_s = mo_s.reshape(dec_batch, kvh, gh, dec_seq, dh).transpose(0, 3, 1, 2, 4).reshape(dec_batch * dec_seq, -1)

    gated = _merge1_call(x, lat_p, lat_s.reshape(-1, heads * kv_lora).astype(BF16),
                         mo_p, mo_s.astype(BF16), row1(lw["g_attn"]), wg, wuv, woa, wob, dims)
    x2, h2, selw = _merge2_call(gated, x, wout, row1(lw["g_ffn"]), wr_hi, wr_lo, br, n_experts)

    route, pos4, w4 = _route(selw[:, :n_experts], n_experts, EXPERT_ROWS)
    yb = _expert_call(route, h2, moe_w, layer)
    return x2, yb, pos4, w4, (ckv32, kpe32, mk32, mv32)


def kernel(x_prompt, x_sample, cache_ckv, cache_kpe, cache_k, cache_v, page_table, g_attn, w_in, g_q, w_q_up, g_kv,
           w_uk, w_uv, w_oa, w_ob, w_out, g_ffn, w_router, b_router, w_gate_up, b_gate_up, w_down, b_down, g_final):
    batch, seq, d = x_prompt.shape
    dec_batch, dec_seq, _ = x_sample.shape
    depth = w_in.shape[0]
    n_pages = page_table.shape[1]
    page = cache_ckv.shape[2]
    past = n_pages * page
    kv_lora, heads, nope = w_uk.shape[1:]
    rope = w_q_up.shape[2] // heads - nope
    kvh, dh = cache_k.shape[3:]
    dims = dict(heads=heads, nope=nope, rope=rope, kv_lora=kv_lora, q_lora=w_q_up.shape[1],
                m_heads=w_ob.shape[1] // dh, m_kvh=kvh, dh=dh, mla_scale=(nope + rope) ** -0.5)
    dims["n_mq"] = dims["m_heads"] * dh
    dims["n_kv"] = kvh * dh
    tp, ts = batch * seq, dec_batch * dec_seq
    assert seq % MOBA_BLOCK == 0 and past % MOBA_BLOCK == 0 and dec_seq <= 16
    assert tp % TOKEN_TILE == 0 and ts % TOKEN_TILE == 0 and n_pages % min(DECODE_PAGES, n_pages) == 0
    assert seq % min(MLA_K_TILE, seq) == 0 and min(MLA_K_TILE, seq) % min(MLA_Q_TOKENS, seq) == 0

    x = jnp.concatenate([x_prompt.reshape(tp, d), x_sample.reshape(ts, d)], axis=0)
    pos = jnp.concatenate([jnp.tile(jnp.arange(seq, dtype=jnp.int32), batch),
                           jnp.tile(past + jnp.arange(dec_seq, dtype=jnp.int32), dec_batch)])
    outs = [[] for _ in range(8)]
    for layer in range(depth):
        lw = dict(g_attn=g_attn[layer], w_in=w_in[layer], g_q=g_q[layer], w_q_up=w_q_up[layer], g_kv=g_kv[layer],
                  w_uk=w_uk[layer], w_uv=w_uv[layer], w_oa=w_oa[layer], w_ob=w_ob[layer], w_out=w_out[layer],
                  g_ffn=g_ffn[layer], w_router=w_router[layer], b_router=b_router[layer])
        moe_w = (w_gate_up, b_gate_up, w_down, b_down)
        x2, yb, pos4, w4, (ckv, kpe, mk, mv) = _layer(
            x, pos, batch, seq, dec_batch, dec_seq, (cache_ckv, cache_kpe, cache_k, cache_v), page_table,
            lw, moe_w, layer, dims)
        final = layer == depth - 1
        x = _combine_call(pos4, w4, x2, g_final.reshape(1, d), yb, final)
        outs[0].append(ckv[:tp].reshape(batch, seq, kv_lora))
        outs[1].append(kpe[:tp].reshape(batch, seq, rope))
        outs[2].append(mk[:tp].reshape(batch, seq, kvh, dh))
        outs[3].append(mv[:tp].reshape(batch, seq, kvh, dh))
        outs[4].append(ckv[tp:].reshape(dec_batch, dec_seq, kv_lora))
        outs[5].append(kpe[tp:].reshape(dec_batch, dec_seq, rope))
        outs[6].append(mk[tp:].reshape(dec_batch, dec_seq, kvh, dh))
        outs[7].append(mv[tp:].reshape(dec_batch, dec_seq, kvh, dh))
    y_prompt = x[:tp].reshape(batch, seq, d)
    y_sample = x[tp:].reshape(dec_batch, dec_seq, d)
    return (y_prompt, y_sample) + tuple(jnp.stack(o) for o in outs)
```

```python
import functools

import jax
import jax.numpy as jnp
from jax import lax
from jax.experimental import pallas as pl
from jax.experimental.pallas import tpu as pltpu

F32 = jnp.float32
BF16 = jnp.bfloat16

NORM_EPS = 1e-6
ROPE_THETA = 10000.0
MOBA_BLOCK = 256
MOBA_TOPK = 3
MOE_TOP_K = 4
SWIGLU_LIMIT = 7.0
SWIGLU_ALPHA = 1.702
NEG_INF = -1e30

LANES = 128
VMEM_LIMIT = 56 * 1024 * 1024

TOKEN_TILE = 256
MLA_Q_TOKENS = 128
MLA_K_TILE = 512
DECODE_PAGES = 32
MOBA_DECODE_PAGES = 32
EXPERT_ROWS = 768
EXPERT_SUB_ROWS = 256
EXPERT_FF = 256
COMBINE_TOKENS = 128


def _rms(x, g):
    return x * lax.rsqrt(jnp.mean(x * x, axis=-1, keepdims=True) + NORM_EPS) * g


def _dot(a, b):
    return jnp.dot(a, b, preferred_element_type=F32)


def _dot_nt(a, b):
    return lax.dot_general(a, b, (((1,), (1,)), ((), ())), preferred_element_type=F32)


def _topk_onehot(vals, k, valid, axis=-1):
    axis = axis % vals.ndim
    n = vals.shape[axis]
    iota = lax.broadcasted_iota(jnp.int32, vals.shape, axis).astype(F32)
    work = jnp.where(valid, vals, -jnp.inf)
    sel = jnp.zeros(vals.shape, F32)
    for _ in range(k):
        m = jnp.max(work, axis=axis, keepdims=True)
        idx = jnp.min(jnp.where(work == m, iota, float(n)), axis=axis, keepdims=True)
        hit = iota == idx
        sel = jnp.where(hit, 1.0, sel)
        work = jnp.where(hit, -jnp.inf, work)
    return jnp.where(valid, sel, 0.0)


def _params(sem, vmem=VMEM_LIMIT):
    return pltpu.CompilerParams(dimension_semantics=sem, vmem_limit_bytes=vmem)


def _const_spec(shape):
    nd = len(shape)
    return pl.BlockSpec(shape, lambda *_: (0,) * nd, pipeline_mode=pl.Buffered(1))


def _proj_body(x_ref, cos_ref, sin_ref, gat_ref, wa_ref, gq_ref, wq_ref, gkv_ref, wuk_ref,
               qabs_ref, qrope_ref, ckv32_ref, ckv16_ref, kpe32_ref, kpe16_ref,
               mq_ref, mk32_ref, mv32_ref, mk16_ref, mv16_ref, kmean_ref, ckvt_ref, mvt_ref,
               *, q_lora, kv_lora, rope, n_mq, n_kv, heads, nope, scale):
    hb = _rms(x_ref[...], gat_ref[...]).astype(BF16)
    off = [0]

    def proj(n):
        r = _dot(hb, wa_ref[:, off[0]:off[0] + n])
        off[0] += n
        return r

    q_lat = proj(q_lora)
    c_raw = proj(kv_lora)
    mq = proj(n_mq)
    mk = proj(n_kv)
    mv = proj(n_kv)
    kpe_a = proj(LANES)
    kpe_b = proj(LANES)

    cos = cos_ref[...]
    sin = sin_ref[...]
    ckv = _rms(c_raw, gkv_ref[...])
    ckv32_ref[...] = ckv
    ckv16_ref[...] = ckv.astype(BF16)
    kpe = kpe_a * cos + kpe_b * sin
    kpe32_ref[...] = kpe[:, :rope]
    kpe16_ref[...] = kpe.astype(BF16)
    mq_ref[...] = mq.astype(BF16)
    mk32_ref[...] = mk
    mv32_ref[...] = mv
    mk16_ref[...] = mk.astype(BF16)
    mv16_ref[...] = mv.astype(BF16)
    kmean_ref[...] = jnp.sum(mk, axis=0, keepdims=True) * (1.0 / MOBA_BLOCK)
    ckvt_ref[...] = ckv.T.astype(BF16)
    mvt_ref[...] = mv.T.astype(BF16)

    qn = _rms(q_lat, gq_ref[...]).astype(BF16)
    hn = heads * nope
    hr = heads * LANES
    q_nope = _dot(qn, wq_ref[:, :hn])
    q_ra = _dot(qn, wq_ref[:, hn:hn + hr])
    q_rb = _dot(qn, wq_ref[:, hn + hr:hn + 2 * hr])
    for h in range(heads):
        lanes = slice(h * LANES, (h + 1) * LANES)
        qrope_ref[:, lanes] = ((q_ra[:, lanes] * cos + q_rb[:, lanes] * sin) * scale).astype(BF16)
        qa = _dot(q_nope[:, h * nope:(h + 1) * nope].astype(BF16), wuk_ref[h])
        qabs_ref[:, h * kv_lora:(h + 1) * kv_lora] = (qa * scale).astype(BF16)


def _proj_call(x, cos_t, sin_t, g_attn, wa, g_q, wq, g_kv, wuk, dims):
    t, d = x.shape
    tm = TOKEN_TILE
    heads, nope, rope = dims["heads"], dims["nope"], dims["rope"]
    q_lora, kv_lora, n_mq, n_kv = dims["q_lora"], dims["kv_lora"], dims["n_mq"], dims["n_kv"]
    nt = t // tm
    row = lambda w: pl.BlockSpec((tm, w), lambda i: (i, 0))
    out_shape = (
        jax.ShapeDtypeStruct((t, heads * kv_lora), BF16),
        jax.ShapeDtypeStruct((t, heads * LANES), BF16),
        jax.ShapeDtypeStruct((t, kv_lora), F32),
        jax.ShapeDtypeStruct((t, kv_lora), BF16),
        jax.ShapeDtypeStruct((t, rope), F32),
        jax.ShapeDtypeStruct((t, LANES), BF16),
        jax.ShapeDtypeStruct((t, n_mq), BF16),
        jax.ShapeDtypeStruct((t, n_kv), F32),
        jax.ShapeDtypeStruct((t, n_kv), F32),
        jax.ShapeDtypeStruct((t, n_kv), BF16),
        jax.ShapeDtypeStruct((t, n_kv), BF16),
        jax.ShapeDtypeStruct((nt, 1, n_kv), F32),
        jax.ShapeDtypeStruct((nt, kv_lora, tm), BF16),
        jax.ShapeDtypeStruct((nt, n_kv, tm), BF16),
    )
    tile3 = lambda w: pl.BlockSpec((None, w, tm), lambda i: (i, 0, 0))
    out_specs = (row(heads * kv_lora), row(heads * LANES), row(kv_lora), row(kv_lora), row(rope), row(LANES),
                 row(n_mq), row(n_kv), row(n_kv), row(n_kv), row(n_kv),
                 pl.BlockSpec((None, 1, n_kv), lambda i: (i, 0, 0)), tile3(kv_lora), tile3(n_kv))
    in_specs = [row(d), row(LANES), row(LANES), _const_spec(g_attn.shape), _const_spec(wa.shape),
                _const_spec(g_q.shape), _const_spec(wq.shape), _const_spec(g_kv.shape), _const_spec(wuk.shape)]
    body = functools.partial(_proj_body, q_lora=q_lora, kv_lora=kv_lora, rope=rope, n_mq=n_mq, n_kv=n_kv,
                             heads=heads, nope=nope, scale=dims["mla_scale"])
    return pl.pallas_call(body, grid=(nt,), in_specs=in_specs, out_specs=out_specs, out_shape=out_shape,
                          compiler_params=_params(("arbitrary",)), name="proj")(
                              x, cos_t, sin_t, g_attn, wa, g_q, wq, g_kv, wuk)


def _mla_prompt_body(qa_ref, qr_ref, ckv_ref, kpe_ref, ckvt_ref, out_ref, m_scr, l_scr, acc_scr,
                     *, tq, tk, heads):
    qi = pl.program_id(1)
    kv_lora = ckv_ref.shape[1]
    sub = ckvt_ref.shape[2]
    qa = jnp.concatenate([qa_ref[:, h * kv_lora:(h + 1) * kv_lora] for h in range(heads)], axis=0)
    qr = jnp.concatenate([qr_ref[:, h * LANES:(h + 1) * LANES] for h in range(heads)], axis=0)
    m_scr[...] = jnp.full(m_scr.shape, NEG_INF, F32)
    l_scr[...] = jnp.zeros(l_scr.shape, F32)
    acc_scr[...] = jnp.zeros(acc_scr.shape, F32)
    tok = qi * tq + jnp.concatenate([lax.broadcasted_iota(jnp.int32, (1, tq), 1)] * heads, axis=1)
    n_k = lax.div(qi * tq, tk) + 1

    def step(kt, carry):
        start = pl.multiple_of(kt * tk, tk)
        kc = ckv_ref[pl.ds(start, tk), :]
        kr = kpe_ref[pl.ds(start, tk), :]
        s = _dot_nt(kc, qa) + _dot_nt(kr, qr)
        key = start + lax.broadcasted_iota(jnp.int32, (tk, 1), 0)
        s = jnp.where(key <= tok, s, NEG_INF)
        m_old = m_scr[...]
        m_new = jnp.maximum(m_old, jnp.max(s, axis=0, keepdims=True))
        alpha = jnp.exp(m_old - m_new)
        p = jnp.exp(s - m_new)
        l_scr[...] = alpha * l_scr[...] + jnp.sum(p, axis=0, keepdims=True)
        pb = p.astype(BF16)
        pv = _dot(ckvt_ref[kt * (tk // sub)], pb[:sub, :])
        for c in range(1, tk // sub):
            pv = pv + _dot(ckvt_ref[kt * (tk // sub) + c], pb[c * sub:(c + 1) * sub, :])
        acc_scr[...] = alpha * acc_scr[...] + pv
        m_scr[...] = m_new
        return carry

    lax.fori_loop(0, n_k, step, 0)
    o_t = acc_scr[...] / l_scr[...]
    for h in range(heads):
        out_ref[:, h * kv_lora:(h + 1) * kv_lora] = o_t[:, h * tq:(h + 1) * tq].T.astype(BF16)


def _mla_prompt_call(qabs, qrope, ckv16, kpe16, ckvt, batch, seq, dims):
    heads, kv_lora = dims["heads"], dims["kv_lora"]
    tq = min(MLA_Q_TOKENS, seq)
    tk = min(MLA_K_TILE, seq)
    nq = seq // tq
    rows = tq * heads
    sub = ckvt.shape[2]
    body = functools.partial(_mla_prompt_body, tq=tq, tk=tk, heads=heads)
    return pl.pallas_call(
        body, grid=(batch, nq),
        in_specs=[pl.BlockSpec((tq, heads * kv_lora), lambda b, q: (b * nq + q, 0)),
                  pl.BlockSpec((tq, heads * LANES), lambda b, q: (b * nq + q, 0)),
                  pl.BlockSpec((seq, kv_lora), lambda b, q: (b, 0)),
                  pl.BlockSpec((seq, LANES), lambda b, q: (b, 0)),
                  pl.BlockSpec((seq // sub, kv_lora, sub), lambda b, q: (b, 0, 0))],
        out_specs=pl.BlockSpec((tq, heads * kv_lora), lambda b, q: (b * nq + q, 0)),
        out_shape=jax.ShapeDtypeStruct((batch * seq, heads * kv_lora), BF16),
        scratch_shapes=[pltpu.VMEM((1, rows), F32), pltpu.VMEM((1, rows), F32), pltpu.VMEM((kv_lora, rows), F32)],
        compiler_params=_params(("arbitrary", "arbitrary")), name="mla_prompt")(qabs, qrope, ckv16, kpe16, ckvt)


def _moba_prompt_body(q_ref, k_ref, vt_ref, kmean_ref, slope_ref, out_ref, sel_scr, m_scr, l_scr, acc_scr,
                      *, gh, dh, nb, scale):
    qi = pl.program_id(2)
    blk = MOBA_BLOCK
    rows = gh * blk
    q = jnp.concatenate([q_ref[:, j * dh:(j + 1) * dh] for j in range(gh)], axis=0)
    slope = slope_ref[...]
    tpos = lax.broadcasted_iota(jnp.int32, (1, blk), 1)
    qpos = qi * blk + jnp.concatenate([tpos] * gh, axis=1)
    qposf = qpos.astype(F32)

    gate = _dot_nt(kmean_ref[...].astype(BF16), q)
    biota = lax.broadcasted_iota(jnp.int32, (nb, rows), 0)
    sel_scr[...] = _topk_onehot(gate, MOBA_TOPK, biota < qi, axis=0)

    def scores(kt):
        start = pl.multiple_of(kt * blk, blk)
        kb = k_ref[pl.ds(start, blk), :]
        kpos = start + lax.broadcasted_iota(jnp.int32, (blk, 1), 0)
        s = _dot_nt(kb, q) * scale - slope * (qposf - kpos.astype(F32))
        return s, vt_ref[kt], kpos

    s, vt, kpos = scores(qi)
    s = jnp.where(kpos <= qpos, s, NEG_INF)
    m0 = jnp.max(s, axis=0, keepdims=True)
    p = jnp.exp(s - m0)
    m_scr[...] = m0
    l_scr[...] = jnp.sum(p, axis=0, keepdims=True)
    acc_scr[...] = _dot(vt, p.astype(BF16))

    def step(kt, carry):
        s, vt, _ = scores(kt)
        chosen = sel_scr[pl.ds(kt, 1), :]
        s = jnp.where(chosen > 0.5, s, NEG_INF)
        m_old = m_scr[...]
        m_new = jnp.maximum(m_old, jnp.max(s, axis=0, keepdims=True))
        alpha = jnp.exp(m_old - m_new)
        p = jnp.exp(s - m_new)
        l_scr[...] = alpha * l_scr[...] + jnp.sum(p, axis=0, keepdims=True)
        acc_scr[...] = alpha * acc_scr[...] + _dot(vt, p.astype(BF16))
        m_scr[...] = m_new
        return carry

    lax.fori_loop(0, qi, step, 0)
    o_t = acc_scr[...] / l_scr[...]
    for j in range(gh):
        out_ref[:, j * dh:(j + 1) * dh] = o_t[:, j * blk:(j + 1) * blk].T.astype(BF16)


def _moba_prompt_call(mq16, mk16, mvt, kmean, slope_rows, batch, seq, dims):
    heads, kvh, dh = dims["m_heads"], dims["m_kvh"], dims["dh"]
    gh = heads // kvh
    blk = MOBA_BLOCK
    nb = seq // blk
    rows = gh * blk
    body = functools.partial(_moba_prompt_body, gh=gh, dh=dh, nb=nb, scale=dh ** -0.5)
    return pl.pallas_call(
        body, grid=(batch, kvh, nb),
        in_specs=[pl.BlockSpec((blk, gh * dh), lambda b, g, q: (b * nb + q, g)),
                  pl.BlockSpec((seq, dh), lambda b, g, q: (b, g)),
                  pl.BlockSpec((nb, dh, blk), lambda b, g, q: (b, g, 0)),
                  pl.BlockSpec((nb, dh), lambda b, g, q: (b, g)),
                  pl.BlockSpec((None, 1, rows), lambda b, g, q: (g, 0, 0))],
        out_specs=pl.BlockSpec((blk, gh * dh), lambda b, g, q: (b * nb + q, g)),
        out_shape=jax.ShapeDtypeStruct((batch * seq, heads * dh), BF16),
        scratch_shapes=[pltpu.VMEM((nb, rows), F32), pltpu.VMEM((1, rows), F32), pltpu.VMEM((1, rows), F32),
                        pltpu.VMEM((dh, rows), F32)],
        compiler_params=_params(("arbitrary", "arbitrary", "arbitrary")), name="moba_prompt")(
            mq16, mk16, mvt, kmean, slope_rows)


def _page_spec(block, layer, pages_per_step, i, n_steps=None, phase=0):
    nd = len(block)

    def index(s, j, pt):
        jj = j - phase
        if n_steps is not None:
            jj = jnp.clip(jj, 0, n_steps - 1)
        return (layer, pt[s, jj * pages_per_step + i]) + (0,) * (nd - 2)

    return pl.BlockSpec(block, index)


def _mla_decode_body(pt_ref, qa_ref, qr_ref, cnew_ref, rnew_ref, *rest, pages, heads, n_new):
    ckv_pages = rest[:pages]
    kpe_pages = rest[pages:2 * pages]
    out_ref, m_scr, l_scr, acc_scr = rest[2 * pages:]
    j = pl.program_id(1)
    qa = qa_ref[...]
    qr = qr_ref[...]
    rows = qa.shape[0]

    @pl.when(j == 0)
    def _():
        cn = cnew_ref[...]
        s = _dot_nt(qa, cn) + _dot_nt(qr, rnew_ref[...])
        tok = lax.div(lax.broadcasted_iota(jnp.int32, (rows, 1), 0), heads)
        key = lax.broadcasted_iota(jnp.int32, (1, cn.shape[0]), 1)
        s = jnp.where((key <= tok) & (key < n_new), s, NEG_INF)
        m0 = jnp.max(s, axis=-1, keepdims=True)
        p = jnp.exp(s - m0)
        m_scr[...] = m0
        l_scr[...] = jnp.sum(p, axis=-1, keepdims=True)
        acc_scr[...] = _dot(p.astype(BF16), cn)

    kc = jnp.concatenate([r[...].astype(BF16) for r in ckv_pages], axis=0)
    kr_t = jnp.concatenate([r[...].astype(BF16) for r in kpe_pages], axis=1)
    s = _dot_nt(qa, kc) + _dot(qr[:, :kr_t.shape[0]], kr_t)
    m_old = m_scr[...]
    m_new = jnp.maximum(m_old, jnp.max(s, axis=-1, keepdims=True))
    alpha = jnp.exp(m_old - m_new)
    p = jnp.exp(s - m_new)
    l_scr[...] = alpha * l_scr[...] + jnp.sum(p, axis=-1, keepdims=True)
    acc_scr[...] = alpha * acc_scr[...] + _dot(p.astype(BF16), kc)
    m_scr[...] = m_new

    @pl.when(j == pl.num_programs(1) - 1)
    def _():
        out_ref[...] = acc_scr[...] / l_scr[...]


def _mla_decode_call(page_table, qa_s, qr_s, cnew, rnew, cache_ckv, cache_kpe, layer, dims, n_new):
    ns, rows, kv_lora = qa_s.shape
    n_pages = page_table.shape[1]
    page = cache_ckv.shape[2]
    rope = cache_kpe.shape[3]
    kpe_t = jnp.swapaxes(cache_kpe, 2, 3)
    pages = min(DECODE_PAGES, n_pages)
    nj = n_pages // pages
    npad = cnew.shape[1]
    seq_spec = lambda r, w: pl.BlockSpec((None, r, w), lambda s, j, pt: (s, 0, 0))
    in_specs = [seq_spec(rows, kv_lora), seq_spec(rows, LANES), seq_spec(npad, kv_lora), seq_spec(npad, LANES)]
    in_specs += [_page_spec((None, None, page, kv_lora), layer, pages, i) for i in range(pages)]
    in_specs += [_page_spec((None, None, rope, page), layer, pages, i) for i in range(pages)]
    body = functools.partial(_mla_decode_body, pages=pages, heads=dims["heads"], n_new=n_new)
    grid_spec = pltpu.PrefetchScalarGridSpec(
        num_scalar_prefetch=1, grid=(ns, nj), in_specs=in_specs, out_specs=seq_spec(rows, kv_lora),
        scratch_shapes=[pltpu.VMEM((rows, 1), F32), pltpu.VMEM((rows, 1), F32), pltpu.VMEM((rows, kv_lora), F32)])
    return pl.pallas_call(body, grid_spec=grid_spec, out_shape=jax.ShapeDtypeStruct((ns, rows, kv_lora), F32),
                          compiler_params=_params(("arbitrary", "arbitrary")), name="mla_decode")(
                              page_table, qa_s, qr_s, cnew, rnew, *([cache_ckv] * pages), *([kpe_t] * pages))


def _moba_decode_body(pt_ref, q_ref, knew_ref, vnew_ref, slope_ref, *rest,
                      pages, page, kvh, dh, n_new, past, scale, n_steps):
    k_pages = rest[:pages]
    v_pages = rest[pages:2 * pages]
    out_ref, s_scr, p_scr, mean_scr, l_scr, acc_scr = rest[2 * pages:]
    j = pl.program_id(1)
    nj = n_steps
    blk = MOBA_BLOCK
    keys = pages * page
    bps = keys // blk
    nblk = mean_scr.shape[1]
    rows = q_ref.shape[1]
    tok = lax.rem(lax.broadcasted_iota(jnp.int32, (rows, 1), 0), n_new)
    qposf = (past + tok).astype(F32)

    def head_rows(page_ref, g):
        return page_ref[pl.ds(g, page, stride=kvh), :]

    @pl.when(j < nj)
    def _():
        for g in range(kvh):
            kg = jnp.concatenate([head_rows(r, g) for r in k_pages], axis=0)
            sums = jnp.sum(kg.reshape(bps, blk, dh), axis=1)
            mean_scr[g, pl.ds(pl.multiple_of(j * bps, bps), bps), :] = sums * (1.0 / blk)
            s_scr[g, j] = _dot_nt(q_ref[g], kg.astype(BF16))

    @pl.when(j == nj)
    def _():
        ci = lax.broadcasted_iota(jnp.int32, (nblk, keys), 1)
        for g in range(kvh):
            q = q_ref[g]
            slope = slope_ref[g]
            gate = _dot_nt(q, mean_scr[g].astype(BF16))
            sel = _topk_onehot(gate, MOBA_TOPK, jnp.full(gate.shape, True)).astype(BF16)
            kn = knew_ref[:, g * dh:(g + 1) * dh]
            key = lax.broadcasted_iota(jnp.int32, (1, kn.shape[0]), 1)
            s_new = _dot_nt(q, kn) * scale - slope * (tok - key).astype(F32)
            s_new = jnp.where((key <= tok) & (key < n_new), s_new, NEG_INF)
            m = jnp.max(s_new, axis=-1, keepdims=True)
            for jj in range(n_steps):
                lo = (lax.broadcasted_iota(jnp.int32, (nblk, keys), 0) - jj * bps) * blk
                expand = jnp.where((ci >= lo) & (ci < lo + blk), 1.0, 0.0).astype(BF16)
                chosen = _dot(sel, expand)
                kpos = jj * keys + lax.broadcasted_iota(jnp.int32, (1, keys), 1)
                s = s_scr[g, jj] * scale - slope * (qposf - kpos.astype(F32))
                s = jnp.where(chosen > 0.5, s, NEG_INF)
                s_scr[g, jj] = s
                m = jnp.maximum(m, jnp.max(s, axis=-1, keepdims=True))
            p_new = jnp.exp(s_new - m)
            l = jnp.sum(p_new, axis=-1, keepdims=True)
            for jj in range(n_steps):
                p = jnp.exp(s_scr[g, jj] - m)
                l = l + jnp.sum(p, axis=-1, keepdims=True)
                p_scr[g, jj] = p.astype(BF16)
            l_scr[g] = l
            acc_scr[g] = _dot(p_new.astype(BF16), vnew_ref[:, g * dh:(g + 1) * dh])

    @pl.when(j >= nj)
    def _():
        jj = j - nj
        for g in range(kvh):
            vg = jnp.concatenate([head_rows(r, g).astype(BF16) for r in v_pages], axis=0)
            acc_scr[g] += _dot(p_scr[g, jj], vg)

    @pl.when(j == 2 * nj - 1)
    def _():
        for g in range(kvh):
            out_ref[g] = acc_scr[g] / l_scr[g]


def _moba_decode_call(page_table, q_s, knew, vnew, slope_rows, cache_k, cache_v, layer, n_new):
    ns, kvh, rows, dh = q_s.shape
    n_pages = page_table.shape[1]
    page = cache_k.shape[2]
    width = kvh * dh
    cache_k = cache_k.reshape(cache_k.shape[:2] + (page * kvh, dh))
    cache_v = cache_v.reshape(cache_v.shape[:2] + (page * kvh, dh))
    pages = min(MOBA_DECODE_PAGES, n_pages)
    nj = n_pages // pages
    past = n_pages * page
    nblk = past // MOBA_BLOCK
    keys = pages * page
    npad = knew.shape[1]
    in_specs = [pl.BlockSpec((None, kvh, rows, dh), lambda s, j, pt: (s, 0, 0, 0)),
                pl.BlockSpec((None, npad, width), lambda s, j, pt: (s, 0, 0)),
                pl.BlockSpec((None, npad, width), lambda s, j, pt: (s, 0, 0)),
                pl.BlockSpec((kvh, rows, 1), lambda s, j, pt: (0, 0, 0))]
    rows_pp = page * kvh
    in_specs += [_page_spec((None, None, rows_pp, dh), layer, pages, i, n_steps=nj, phase=0) for i in range(pages)]
    in_specs += [_page_spec((None, None, rows_pp, dh), layer, pages, i, n_steps=nj, phase=nj) for i in range(pages)]
    body = functools.partial(_moba_decode_body, pages=pages, page=page, kvh=kvh, dh=dh, n_new=n_new, past=past,
                             scale=dh ** -0.5, n_steps=nj)
    grid_spec = pltpu.PrefetchScalarGridSpec(
        num_scalar_prefetch=1, grid=(ns, 2 * nj), in_specs=in_specs,
        out_specs=pl.BlockSpec((None, kvh, rows, dh), lambda s, j, pt: (s, 0, 0, 0)),
        scratch_shapes=[pltpu.VMEM((kvh, nj, rows, keys), F32), pltpu.VMEM((kvh, nj, rows, keys), BF16),
                        pltpu.VMEM((kvh, nblk, dh), F32), pltpu.VMEM((kvh, rows, 1), F32),
                        pltpu.VMEM((kvh, rows, dh), F32)])
    return pl.pallas_call(body, grid_spec=grid_spec, out_shape=jax.ShapeDtypeStruct((ns, kvh, rows, dh), F32),
                          compiler_params=_params(("arbitrary", "arbitrary")), name="moba_decode")(
                              page_table, q_s, knew, vnew, slope_rows, *([cache_k] * pages), *([cache_v] * pages))


def _merge1_body(x_ref, latp_ref, lats_ref, mop_ref, mos_ref, gat_ref, wg_ref, wuv_ref, woa_ref, wob_ref, out_ref,
                 *, heads, kv_lora, prompt_tiles):
    d = x_ref.shape[1]
    is_prompt = pl.program_id(0) < prompt_tiles
    lat = jnp.where(is_prompt, latp_ref[...], lats_ref[...])
    mo = jnp.where(is_prompt, mop_ref[...], mos_ref[...])
    hb = _rms(x_ref[...], gat_ref[...]).astype(BF16)
    ga = _dot(hb, wg_ref[:, :d])
    gb = _dot(hb, wg_ref[:, d:])
    a_in = jnp.concatenate(
        [_dot(lat[:, h * kv_lora:(h + 1) * kv_lora], wuv_ref[h]).astype(BF16) for h in range(heads)], axis=1)
    a = _dot(a_in, woa_ref[...])
    m = _dot(mo, wob_ref[...])
    out_ref[...] = (jax.nn.sigmoid(ga) * a + jax.nn.sigmoid(gb) * m).astype(BF16)


def _merge1_call(x, lat_p, lat_s, mo_p, mo_s, g_attn, wg, wuv, woa, wob, dims):
    t, d = x.shape
    tm = TOKEN_TILE
    npt = lat_p.shape[0] // tm
    row = lambda w: pl.BlockSpec((tm, w), lambda i: (i, 0))
    prow = lambda w: pl.BlockSpec((tm, w), lambda i: (jnp.minimum(i, npt - 1), 0))
    srow = lambda w: pl.BlockSpec((tm, w), lambda i: (jnp.maximum(i - npt, 0), 0))
    body = functools.partial(_merge1_body, heads=dims["heads"], kv_lora=dims["kv_lora"], prompt_tiles=npt)
    return pl.pallas_call(
        body, grid=(t // tm,),
        in_specs=[row(d), prow(lat_p.shape[1]), srow(lat_s.shape[1]), prow(mo_p.shape[1]), srow(mo_s.shape[1]),
                  _const_spec(g_attn.shape), _const_spec(wg.shape),
                  _const_spec(wuv.shape), _const_spec(woa.shape), _const_spec(wob.shape)],
        out_specs=row(d), out_shape=jax.ShapeDtypeStruct((t, d), BF16),
        compiler_params=_params(("arbitrary",)), name="merge1")(
            x, lat_p, lat_s, mo_p, mo_s, g_attn, wg, wuv, woa, wob)


def _merge2_body(gated_ref, x_ref, wout_ref, gffn_ref, wr_hi_ref, wr_lo_ref, br_ref, x2_ref, h2_ref, selw_ref,
                 *, n_experts):
    x2 = x_ref[...] + _dot(gated_ref[...], wout_ref[...])
    x2_ref[...] = x2
    h2 = _rms(x2, gffn_ref[...])
    h2_ref[...] = h2
    h_hi = h2.astype(BF16)
    h_lo = (h2 - h_hi.astype(F32)).astype(BF16)
    logits = _dot(h_hi, wr_hi_ref[...]) + _dot(h_hi, wr_lo_ref[...]) + _dot(h_lo, wr_hi_ref[...]) + br_ref[...]
    valid = lax.broadcasted_iota(jnp.int32, logits.shape, 1) < n_experts
    sel = _topk_onehot(logits, MOE_TOP_K, valid)
    top = jnp.max(jnp.where(valid, logits, -jnp.inf), axis=-1, keepdims=True)
    e = jnp.where(sel > 0.0, jnp.exp(logits - top), 0.0)
    w = e / jnp.sum(e, axis=-1, keepdims=True)
    selw_ref[...] = jnp.where(sel > 0.0, w, -1.0)


def _merge2_call(gated, x, wout, g_ffn, wr_hi, wr_lo, br, n_experts):
    t, d = x.shape
    tm = TOKEN_TILE
    row = lambda w: pl.BlockSpec((tm, w), lambda i: (i, 0))
    body = functools.partial(_merge2_body, n_experts=n_experts)
    return pl.pallas_call(
        body, grid=(t // tm,),
        in_specs=[row(d), row(d), _const_spec(wout.shape), _const_spec(g_ffn.shape), _const_spec(wr_hi.shape),
                  _const_spec(wr_lo.shape), _const_spec(br.shape)],
        out_specs=(row(d), row(d), row(LANES)),
        out_shape=(jax.ShapeDtypeStruct((t, d), F32), jax.ShapeDtypeStruct((t, d), F32),
                   jax.ShapeDtypeStruct((t, LANES), F32)),
        compiler_params=_params(("arbitrary",)), name="merge2")(gated, x, wout, g_ffn, wr_hi, wr_lo, br)


def _row_copy(src_hbm, src_row, dst, dst_row, sem):
    return pltpu.make_async_copy(src_hbm.at[pl.ds(src_row, 1), :], dst.at[pl.ds(dst_row, 1), :], sem)


def _tile_copy(src_hbm, dst, sem):
    return pltpu.make_async_copy(src_hbm.at[pl.ds(0, dst.shape[0]), :], dst, sem)


def _expert_body(te_ref, nu_ref, tb_ref, tv_ref, cur_a, cur_b, nxt_a, nxt_b, h_hbm,
                 wg_ref, wu_ref, bg_ref, bu_ref, wd_ref, bd_ref, out_ref, xg_scr, xb_scr, sem, *, sub_rows):
    i = pl.program_id(0)
    f = pl.program_id(1)
    n_used = nu_ref[0]
    tm = xb_scr.shape[0]

    def gather(tile_idx, blk_a, blk_b):
        off = lax.rem(tb_ref[tile_idx], tm)
        valid = tv_ref[tile_idx]

        def one(r, carry):
            idx = off + r
            tok = jnp.where(idx < tm, blk_a[0, jnp.minimum(idx, tm - 1)], blk_b[0, jnp.maximum(idx - tm, 0)])
            tok = jnp.where(r < valid, tok, 0)
            _row_copy(h_hbm, tok, xg_scr, r, sem).start()
            return carry

        lax.fori_loop(0, tm, one, 0, unroll=8)

    @pl.when((f == 0) & (i == 0))
    def _():
        gather(0, cur_a, cur_b)

    @pl.when((f == 0) & (i < n_used))
    def _():
        _tile_copy(h_hbm, xg_scr, sem).wait()
        xb_scr[...] = xg_scr[...].astype(BF16)
        out_ref[...] = jnp.broadcast_to(bd_ref[...], out_ref.shape)

    @pl.when((f == 0) & (i + 1 < n_used))
    def _():
        gather(i + 1, nxt_a, nxt_b)

    @pl.when((f == 0) & (i >= n_used))
    def _():
        out_ref[...] = jnp.zeros(out_ref.shape, F32)

    @pl.when(i < n_used)
    def _():
        wg = wg_ref[...].astype(BF16)
        wu = wu_ref[...].astype(BF16)
        wd = wd_ref[...].astype(BF16)
        bg = bg_ref[...]
        bu = bu_ref[...]
        for r0 in range(0, tm, sub_rows):
            rows = slice(r0, r0 + sub_rows)
            xb = xb_scr[rows, :]
            gate = jnp.minimum(_dot(xb, wg) + bg, SWIGLU_LIMIT)
            up = jnp.clip(_dot(xb, wu) + bu, -SWIGLU_LIMIT, SWIGLU_LIMIT)
            act = (up + 1.0) * gate * jax.nn.sigmoid(SWIGLU_ALPHA * gate)
            out_ref[rows, :] += _dot(act.astype(BF16), wd)


def _expert_call(route, h2, moe_w, layer):
    tile_expert, n_used, tile_base, tile_valid, sorted_blocks = route
    w_gate_up, b_gate_up, w_down, b_down = moe_w
    d = h2.shape[1]
    depth, n_exp, _, two_f = w_gate_up.shape
    ff = two_f // 2
    tm = EXPERT_ROWS
    tf = min(EXPERT_FF, ff)
    nf = ff // tf
    nt = tile_expert.shape[0]
    nblk = sorted_blocks.shape[0]
    bgu = b_gate_up.reshape(depth, n_exp, 1, two_f)
    bd = b_down.reshape(depth, n_exp, 1, d)

    def tile(i, nu):
        return jnp.minimum(i, nu[0] - 1)

    def expert(i, te, nu):
        return te[tile(i, nu)]

    def col(i, f, nu):
        return jnp.where(i < nu[0], f, nf - 1)

    def tok_block(ahead, second):
        def index(i, f, te, nu, tb, tv):
            blk = lax.div(tb[tile(i + ahead, nu)], tm) + second
            return (jnp.minimum(blk, nblk - 1), 0, 0)
        return pl.BlockSpec((None, 1, tm), index, memory_space=pltpu.SMEM)

    wspec = lambda shape, index: pl.BlockSpec((None, None) + shape, index)
    in_specs = [
        tok_block(0, 0), tok_block(0, 1), tok_block(1, 0), tok_block(1, 1),
        pl.BlockSpec(memory_space=pl.ANY),
        wspec((d, tf), lambda i, f, te, nu, tb, tv: (layer, expert(i, te, nu), 0, col(i, f, nu))),
        wspec((d, tf), lambda i, f, te, nu, tb, tv: (layer, expert(i, te, nu), 0, nf + col(i, f, nu))),
        wspec((1, tf), lambda i, f, te, nu, tb, tv: (layer, expert(i, te, nu), 0, col(i, f, nu))),
        wspec((1, tf), lambda i, f, te, nu, tb, tv: (layer, expert(i, te, nu), 0, nf + col(i, f, nu))),
        wspec((tf, d), lambda i, f, te, nu, tb, tv: (layer, expert(i, te, nu), col(i, f, nu), 0)),
        wspec((1, d), lambda i, f, te, nu, tb, tv: (layer, expert(i, te, nu), 0, 0)),
    ]
    grid_spec = pltpu.PrefetchScalarGridSpec(
        num_scalar_prefetch=4, grid=(nt, nf), in_specs=in_specs,
        out_specs=pl.BlockSpec((tm, d), lambda i, f, te, nu, tb, tv: (i, 0)),
        scratch_shapes=[pltpu.VMEM((tm, d), F32), pltpu.VMEM((tm, d), BF16), pltpu.SemaphoreType.DMA(())])
    body = functools.partial(_expert_body, sub_rows=min(EXPERT_SUB_ROWS, tm))
    return pl.pallas_call(
        body, grid_spec=grid_spec, out_shape=jax.ShapeDtypeStruct((nt * tm, d), F32),
        compiler_params=_params(("arbitrary", "arbitrary")), name="moe_experts")(
            tile_expert, n_used, tile_base, tile_valid, sorted_blocks, sorted_blocks, sorted_blocks, sorted_blocks,
            h2, w_gate_up, w_gate_up, bgu, bgu, w_down, bd)


def _combine_body(pos_cur_ref, pos_next_ref, w_ref, x2_ref, gf_ref, yb_hbm, y_ref, buf, sem, *, tm, top_k, final):
    i = pl.program_id(0)
    n = pl.num_programs(0)
    slot = lax.rem(i, 2)

    def issue(pos_ref, s):
        for k in range(top_k):
            def one(r, carry, k=k):
                _row_copy(yb_hbm, pos_ref[0, k * tm + r], buf.at[s, k], r, sem.at[s]).start()
                return carry
            lax.fori_loop(0, tm, one, 0, unroll=8)

    @pl.when(i == 0)
    def _():
        issue(pos_cur_ref, 0)

    @pl.when(i + 1 < n)
    def _():
        issue(pos_next_ref, 1 - slot)

    for k in range(top_k):
        _tile_copy(yb_hbm, buf.at[slot, k], sem.at[slot]).wait()

    w = w_ref[...]
    moe = w[:, 0:1] * buf[slot, 0]
    for k in range(1, top_k):
        moe = moe + w[:, k:k + 1] * buf[slot, k]
    y = x2_ref[...] + moe
    y_ref[...] = _rms(y, gf_ref[...]) if final else y


def _combine_call(pos4, w4, x2, g_final, yb, final):
    t, d = x2.shape
    top_k = pos4.shape[1]
    tm = COMBINE_TOKENS
    nt = t // tm
    pos_tiles = pos4.reshape(nt, tm, top_k).transpose(0, 2, 1).reshape(nt, 1, top_k * tm)
    body = functools.partial(_combine_body, tm=tm, top_k=top_k, final=final)
    smem = lambda index: pl.BlockSpec((None, 1, top_k * tm), index, memory_space=pltpu.SMEM)
    return pl.pallas_call(
        body, grid=(nt,),
        in_specs=[smem(lambda i: (i, 0, 0)), smem(lambda i: (jnp.minimum(i + 1, nt - 1), 0, 0)),
                  pl.BlockSpec((tm, top_k), lambda i: (i, 0)), pl.BlockSpec((tm, d), lambda i: (i, 0)),
                  _const_spec(g_final.shape), pl.BlockSpec(memory_space=pl.ANY)],
        out_specs=pl.BlockSpec((tm, d), lambda i: (i, 0)),
        out_shape=jax.ShapeDtypeStruct((t, d), F32),
        scratch_shapes=[pltpu.VMEM((2, top_k, tm, d), F32), pltpu.SemaphoreType.DMA((2,))],
        compiler_params=_params(("arbitrary",)), name="moe_combine")(pos_tiles, pos_tiles, w4, x2, g_final, yb)


def _route(selw, n_experts, tile_rows):
    t = selw.shape[0]
    i32 = jnp.int32
    chosen = selw >= 0.0
    sel = chosen.astype(i32)
    counts = jnp.sum(sel, axis=0)
    tiles_per = (counts + tile_rows - 1) // tile_rows
    tile_end = jnp.cumsum(tiles_per)
    tile_start = tile_end - tiles_per
    n_used = tile_end[-1]
    starts = jnp.cumsum(counts) - counts
    n_pairs = t * MOE_TOP_K
    n_tiles = -(-n_pairs // tile_rows) + n_experts
    ti = jnp.arange(n_tiles, dtype=i32)
    tile_expert = jnp.minimum(jnp.sum((ti[:, None] >= tile_end[None, :]).astype(i32), axis=1), n_experts - 1)
    is_e = tile_expert[:, None] == jnp.arange(n_experts, dtype=i32)[None, :]
    pick = lambda v: jnp.sum(jnp.where(is_e, v[None, :], 0), axis=1)
    tile_in_e = ti - pick(tile_start)
    tile_base = pick(starts) + tile_in_e * tile_rows
    tile_valid = jnp.clip(pick(counts) - tile_in_e * tile_rows, 0, tile_rows)
    pos = (tile_start * tile_rows)[None, :] + jnp.cumsum(sel, axis=0) - 1
    kth = jnp.cumsum(sel, axis=1)
    e_iota = jnp.arange(n_experts, dtype=i32)[None, :]
    pos4, w4, e4 = [], [], []
    for k in range(MOE_TOP_K):
        m = chosen & (kth == k + 1)
        pos4.append(jnp.sum(jnp.where(m, pos, 0), axis=1))
        w4.append(jnp.sum(jnp.where(m, selw, 0.0), axis=1))
        e4.append(jnp.sum(jnp.where(m, e_iota, 0), axis=1))
    pos4, w4, e4 = (jnp.stack(a, axis=1) for a in (pos4, w4, e4))
    order = jnp.argsort(e4.reshape(-1))
    sorted_tok = (order // MOE_TOP_K).astype(i32)
    n_blocks = -(-n_pairs // tile_rows) + 1
    sorted_blocks = jnp.pad(sorted_tok, (0, n_blocks * tile_rows - n_pairs)).reshape(n_blocks, 1, tile_rows)
    route = (tile_expert.astype(i32), n_used.reshape(1).astype(i32), tile_base.astype(i32),
             tile_valid.astype(i32), sorted_blocks)
    return route, pos4.astype(i32), w4


def _rope_tables(pos, rope):
    half = rope // 2
    inv = ROPE_THETA ** (-jnp.arange(half, dtype=F32) / half)
    ang = pos.astype(F32)[:, None] * inv[None, :]
    cos, sin = jnp.cos(ang), jnp.sin(ang)
    zero = jnp.zeros((pos.shape[0], LANES - rope), F32)
    return jnp.concatenate([cos, cos, zero], axis=1), jnp.concatenate([-sin, sin, zero], axis=1)


def _pad_lanes(w):
    return jnp.pad(w, [(0, 0)] * (w.ndim - 1) + [(0, LANES - w.shape[-1])])


def _swap_halves(w, rope):
    half = rope // 2
    return jnp.concatenate([w[..., half:], w[..., :half]], axis=-1)


def _layer(x, pos, batch, seq, dec_batch, dec_seq, caches, page_table, lw, moe_w, layer, dims):
    cache_ckv, cache_kpe, cache_k, cache_v = caches
    t, d = x.shape
    heads, nope, rope, kv_lora = dims["heads"], dims["nope"], dims["rope"], dims["kv_lora"]
    m_heads, kvh, dh = dims["m_heads"], dims["m_kvh"], dims["dh"]
    tp = batch * seq

    w_in = lw["w_in"]
    sizes = (dims["q_lora"], kv_lora, rope, m_heads * dh, kvh * dh, kvh * dh, d, d)
    offs = [0]
    for n in sizes:
        offs.append(offs[-1] + n)
    col = lambda k: w_in[:, offs[k]:offs[k + 1]]
    wa = jnp.concatenate([col(0), col(1), col(3), col(4), col(5), _pad_lanes(col(2)),
                          _pad_lanes(_swap_halves(col(2), rope))], axis=1).astype(BF16)
    wg = jnp.concatenate([col(6), col(7)], axis=1).astype(BF16)
    wq3 = lw["w_q_up"].reshape(dims["q_lora"], heads, nope + rope)
    wq_rope = wq3[:, :, nope:]
    wq = jnp.concatenate([wq3[:, :, :nope].reshape(-1, heads * nope),
                          _pad_lanes(wq_rope).reshape(-1, heads * LANES),
                          _pad_lanes(_swap_halves(wq_rope, rope)).reshape(-1, heads * LANES)], axis=1).astype(BF16)
    wuk = jnp.transpose(lw["w_uk"], (1, 2, 0)).astype(BF16)
    wuv = jnp.transpose(lw["w_uv"], (1, 0, 2)).astype(BF16)
    woa = lw["w_oa"].astype(BF16)
    wob = lw["w_ob"].astype(BF16)
    wout = lw["w_out"].astype(BF16)
    n_experts = lw["w_router"].shape[1]
    wr = jnp.pad(lw["w_router"], ((0, 0), (0, LANES - n_experts)))
    wr_hi = wr.astype(BF16)
    wr_lo = (wr - wr_hi.astype(F32)).astype(BF16)
    br = jnp.pad(lw["b_router"], (0, LANES - n_experts)).reshape(1, LANES)
    row1 = lambda g: g.reshape(1, -1)

    cos_t, sin_t = _rope_tables(pos, rope)
    (qabs, qrope, ckv32, ckv16, kpe32, kpe16, mq16, mk32, mv32, mk16, mv16, kmean, ckvt, mvt) = _proj_call(
        x, cos_t, sin_t, row1(lw["g_attn"]), wa, row1(lw["g_q"]), wq, row1(lw["g_kv"]), wuk, dims)
    kmean = kmean.reshape(t // TOKEN_TILE, kvh * dh)

    gh = m_heads // kvh
    slopes = 2.0 ** (-(8.0 / m_heads) * jnp.arange(1, m_heads + 1, dtype=F32))
    slope_p = jnp.repeat(slopes.reshape(kvh, gh), MOBA_BLOCK, axis=1).reshape(kvh, 1, gh * MOBA_BLOCK)
    lat_p = _mla_prompt_call(qabs, qrope, ckv16, kpe16, ckvt, batch, seq, dims)
    mo_p = _moba_prompt_call(mq16, mk16, mvt, kmean, slope_p, batch, seq, dims)

    npad = 16
    pad_new = lambda a: jnp.pad(a[tp:].reshape(dec_batch, dec_seq, -1), ((0, 0), (0, npad - dec_seq), (0, 0)))
    qa_s = qabs[tp:].reshape(dec_batch, dec_seq * heads, kv_lora)
    qr_s = qrope[tp:].reshape(dec_batch, dec_seq * heads, LANES)
    lat_s = _mla_decode_call(page_table, qa_s, qr_s, pad_new(ckv16), pad_new(kpe16), cache_ckv, cache_kpe,
                             layer, dims, dec_seq)
    q_m = mq16[tp:].reshape(dec_batch, dec_seq, kvh, gh, dh).transpose(0, 2, 3, 1, 4)
    q_m = q_m.reshape(dec_batch, kvh, gh * dec_seq, dh)
    slope_s = jnp.repeat(slopes.reshape(kvh, gh), dec_seq, axis=1).reshape(kvh, gh * dec_seq, 1)
    mo_s = _moba_decode_call(page_table, q_m, pad_new(mk16), pad_new(mv16), slope_s, cache_k, cache_v,
                             layer, dec_seq)
    mo_s = mo_s.reshape(dec_batch, kvh, gh, dec_seq, dh).transpose(0, 3, 1, 2, 4).reshape(dec_batch * dec_seq, -1)

    gated = _merge1_call(x, lat_p, lat_s.reshape(-1, heads * kv_lora).astype(BF16),
                         mo_p, mo_s.astype(BF16), row1(lw["g_attn"]), wg, wuv, woa, wob, dims)
    x2, h2, selw = _merge2_call(gated, x, wout, row1(lw["g_ffn"]), wr_hi, wr_lo, br, n_experts)

    route, pos4, w4 = _route(selw[:, :n_experts], n_experts, EXPERT_ROWS)
    yb = _expert_call(route, h2, moe_w, layer)
    return x2, yb, pos4, w4, (ckv32, kpe32, mk32, mv32)


def kernel(x_prompt, x_sample, cache_ckv, cache_kpe, cache_k, cache_v, page_table, g_attn, w_in, g_q, w_q_up, g_kv,
           w_uk, w_uv, w_oa, w_ob, w_out, g_ffn, w_router, b_router, w_gate_up, b_gate_up, w_down, b_down, g_final):
    batch, seq, d = x_prompt.shape
    dec_batch, dec_seq, _ = x_sample.shape
    depth = w_in.shape[0]
    n_pages = page_table.shape[1]
    page = cache_ckv.shape[2]
    past = n_pages * page
    kv_lora, heads, nope = w_uk.shape[1:]
    rope = w_q_up.shape[2] // heads - nope
    kvh, dh = cache_k.shape[3:]
    dims = dict(heads=heads, nope=nope, rope=rope, kv_lora=kv_lora, q_lora=w_q_up.shape[1],
                m_heads=w_ob.shape[1] // dh, m_kvh=kvh, dh=dh, mla_scale=(nope + rope) ** -0.5)
    dims["n_mq"] = dims["m_heads"] * dh
    dims["n_kv"] = kvh * dh
    tp, ts = batch * seq, dec_batch * dec_seq
    assert seq % MOBA_BLOCK == 0 and past % MOBA_BLOCK == 0 and dec_seq <= 16
    assert tp % TOKEN_TILE == 0 and ts % TOKEN_TILE == 0 and n_pages % min(DECODE_PAGES, n_pages) == 0
    assert seq % min(MLA_K_TILE, seq) == 0 and min(MLA_K_TILE, seq) % min(MLA_Q_TOKENS, seq) == 0

    x = jnp.concatenate([x_prompt.reshape(tp, d), x_sample.reshape(ts, d)], axis=0)
    pos = jnp.concatenate([jnp.tile(jnp.arange(seq, dtype=jnp.int32), batch),
                           jnp.tile(past + jnp.arange(dec_seq, dtype=jnp.int32), dec_batch)])
    outs = [[] for _ in range(8)]
    for layer in range(depth):
        lw = dict(g_attn=g_attn[layer], w_in=w_in[layer], g_q=g_q[layer], w_q_up=w_q_up[layer], g_kv=g_kv[layer],
                  w_uk=w_uk[layer], w_uv=w_uv[layer], w_oa=w_oa[layer], w_ob=w_ob[layer], w_out=w_out[layer],
                  g_ffn=g_ffn[layer], w_router=w_router[layer], b_router=b_router[layer])
        moe_w = (w_gate_up, b_gate_up, w_down, b_down)
        x2, yb, pos4, w4, (ckv, kpe, mk, mv) = _layer(
            x, pos, batch, seq, dec_batch, dec_seq, (cache_ckv, cache_kpe, cache_k, cache_v), page_table,
            lw, moe_w, layer, dims)
        final = layer == depth - 1
        x = _combine_call(pos4, w4, x2, g_final.reshape(1, d), yb, final)
        outs[0].append(ckv[:tp].reshape(batch, seq, kv_lora))
        outs[1].append(kpe[:tp].reshape(batch, seq, rope))
        outs[2].append(mk[:tp].reshape(batch, seq, kvh, dh))
        outs[3].append(mv[:tp].reshape(batch, seq, kvh, dh))
        outs[4].append(ckv[tp:].reshape(dec_batch, dec_seq, kv_lora))
        outs[5].append(kpe[tp:].reshape(dec_batch, dec_seq, rope))
        outs[6].append(mk[tp:].reshape(dec_batch, dec_seq, kvh, dh))
        outs[7].append(mv[tp:].reshape(dec_batch, dec_seq, kvh, dh))
    y_prompt = x[:tp].reshape(batch, seq, d)
    y_sample = x[tp:].reshape(dec_batch, dec_seq, d)
    return (y_prompt, y_sample) + tuple(jnp.stack(o) for o in outs)
```
